```python
import jax, jax.numpy as jnp
from jax import lax
import numpy as np

D_MODEL = 1024
BATCH = 2
SEQ = 8192
DEPTH = 4

GRID_W = 64
CTX_LEN = 256
N_MIXERS = 2
N_A = (DEPTH + 1) // 2
N_B = DEPTH // 2
N_MOD = 9
D_FF = 2816
EPS = 1e-6
A_HEADS = 4
A_INNER = 2 * D_MODEL
A_HEAD_DIM = A_INNER // A_HEADS
A_QKV_BLOCK = 4
A_CHUNK = 64
A_CONV = 3
B_CHUNK = 128
B_HALF = 3 * D_MODEL
B_GROUPS = 8

kernel_name = "hybrid_mlstm_gmlp_dit_trunk"


def rmsnorm(x, g):
    xf = x.astype(jnp.float32)
    y = xf * lax.rsqrt(jnp.mean(xf * xf, axis=-1, keepdims=True) + EPS)
    return (y * g.astype(jnp.float32)).astype(x.dtype)


def layernorm(x, g, b):
    xf = x.astype(jnp.float32)
    mu = jnp.mean(xf, axis=-1, keepdims=True)
    var = jnp.mean(jnp.square(xf - mu), axis=-1, keepdims=True)
    y = (xf - mu) * lax.rsqrt(var + EPS)
    return (y * g.astype(jnp.float32) + b.astype(jnp.float32)).astype(x.dtype)


def adaln(cond, w, b):
    mod = jax.nn.silu(cond) @ w + b
    return jnp.split(mod[..., None, :], N_MOD, axis=-1)


def modulate(h, g, shift, scale):
    return rmsnorm(h, g) * (1 + scale) + shift


def residual(h, y, g, gate, weight):
    return h + weight * gate * rmsnorm(y, g)


def swiglu(t, w_in, w_out):
    gate, up = jnp.split(t @ w_in, 2, axis=-1)
    return (jax.nn.silu(gate) * up) @ w_out


def dwconv2d(t, w, bias, rows, cols):
    B, T, Ch = t.shape
    img = t.reshape(B, rows, cols, Ch)
    out = lax.conv_general_dilated(img, w[:, :, None, :], (1, 1), 'SAME',
                                   dimension_numbers=('NHWC', 'HWIO', 'NHWC'),
                                   feature_group_count=Ch)
    return out.reshape(B, T, Ch) + bias


def blockdiag(t, w):
    B, T, _ = t.shape
    return jnp.einsum('btni,noi->btno', t.reshape(B, T, w.shape[0], w.shape[2]), w).reshape(B, T, -1)


def mlstm_scan(q, k, v, ig, lf, state):
    B, H, T, DH = q.shape
    nc = T // A_CHUNK

    def chunks(a):
        return jnp.moveaxis(a.reshape(B, H, nc, A_CHUNK, *a.shape[3:]), 2, 0)

    tril = jnp.tril(jnp.ones((A_CHUNK, A_CHUNK), dtype=bool))

    def step(carry, xs):
        C, n, m = carry
        qc, kc, vc, ic, fc = xs
        b = jnp.cumsum(fc, axis=-1)
        log_d = jnp.where(tril, b[..., :, None] - b[..., None, :] + ic[..., None, :], -jnp.inf)
        m_inter = b + m[..., None]
        m_t = jnp.maximum(m_inter, jnp.max(log_d, axis=-1))
        s = jnp.einsum('bhtd,bhsd->bhts', qc, kc) * jnp.exp(log_d - m_t[..., None])
        scale_inter = jnp.exp(m_inter - m_t)
        num = jnp.einsum('bhts,bhsd->bhtd', s, vc) + scale_inter[..., None] * jnp.einsum('bhvk,bhtk->bhtv', C, qc)
        den = jnp.sum(s, axis=-1) + scale_inter * jnp.einsum('bhk,bhtk->bht', n, qc)
        h = num / jnp.maximum(jnp.abs(den), jnp.exp(-m_t))[..., None]
        b_last = b[..., -1]
        log_w = b_last[..., None] - b + ic
        m_new = jnp.maximum(b_last + m, jnp.max(log_w, axis=-1))
        w = jnp.exp(log_w - m_new[..., None])
        decay = jnp.exp(b_last + m - m_new)
        C_new = decay[..., None, None] * C + jnp.einsum('bhsv,bhsk->bhvk', w[..., None] * vc, kc)
        n_new = decay[..., None] * n + jnp.einsum('bhs,bhsk->bhk', w, kc)
        return (C_new, n_new, m_new), h

    state, hs = lax.scan(step, state, (chunks(q), chunks(k), chunks(v), chunks(ig), chunks(lf)))
    return jnp.moveaxis(hs, 0, 2).reshape(B, H, T, DH), state


def mlstm_inputs(t, rows, cols, w_in, conv_w, conv_b, w_qkv, w_gates, b_gates):
    B, T, _ = t.shape
    xm, z = jnp.split(t @ w_in, 2, axis=-1)
    xc = jax.nn.silu(dwconv2d(xm, conv_w, conv_b, rows, cols))
    q = blockdiag(xc, w_qkv[0])
    k = blockdiag(xc, w_qkv[1])
    v = blockdiag(xm, w_qkv[2])
    gates = jnp.concatenate([q, k, v], axis=-1) @ w_gates + b_gates
    g = jnp.transpose(gates.astype(jnp.float32).reshape(B, T, 4, A_HEADS), (2, 0, 3, 1))

    def heads(a):
        return a.reshape(B, T, A_HEADS, A_HEAD_DIM).transpose(0, 2, 1, 3).astype(jnp.float32)

    qkv = (heads(q), heads(k) * (A_HEAD_DIM ** -0.5), heads(v))
    fwd = qkv + (g[0], jax.nn.log_sigmoid(g[1]))
    bwd = tuple(jnp.flip(a, axis=2) for a in qkv + (g[2], jax.nn.log_sigmoid(g[3])))
    return fwd, bwd, xc, z


def mlstm_output(h, xc, z, norm_g, skip, w_out):
    B, H, T, DH = h.shape
    mu = jnp.mean(h, axis=-1, keepdims=True)
    var = jnp.mean(jnp.square(h - mu), axis=-1, keepdims=True)
    hn = ((h - mu) * lax.rsqrt(var + EPS)).transpose(0, 2, 1, 3).reshape(B, T, H * DH)
    hn = (hn * norm_g.astype(jnp.float32)).astype(xc.dtype)
    return ((hn + skip * xc) * jax.nn.silu(z)) @ w_out


def mlstm_mixer(xn, cn, rows, w_in, conv_w, conv_b, w_qkv, w_gates, b_gates, norm_g, skip, w_out, ctx_out):
    B = xn.shape[0]
    fx, bx, xcx, zx = mlstm_inputs(xn, rows, GRID_W, w_in, conv_w, conv_b, w_qkv, w_gates, b_gates)
    fc, bc, xcc, zc = mlstm_inputs(cn, 1, cn.shape[1], w_in, conv_w, conv_b, w_qkv, w_gates, b_gates)
    zero = (jnp.zeros((B, A_HEADS, A_HEAD_DIM, A_HEAD_DIM), jnp.float32),
            jnp.zeros((B, A_HEADS, A_HEAD_DIM), jnp.float32),
            jnp.zeros((B, A_HEADS), jnp.float32))
    h_cf, s_f = mlstm_scan(*fc, zero)
    h_cb, s_b = mlstm_scan(*bc, zero)
    h_xf, _ = mlstm_scan(*fx, s_f)
    h_xb, _ = mlstm_scan(*bx, s_b)
    y_x = mlstm_output(h_xf + jnp.flip(h_xb, axis=2), xcx, zx, norm_g, skip, w_out).astype(xn.dtype)
    y_c = None
    if ctx_out:
        y_c = mlstm_output(h_cf + jnp.flip(h_cb, axis=2), xcc, zc, norm_g, skip, w_out).astype(cn.dtype)
    return y_x, y_c


def gmlp(t, w_in, b_in, ln_g, ln_b, w_s, b_s, w_out, b_out):
    B, T, _ = t.shape
    u, v = jnp.split(jax.nn.gelu(t @ w_in + b_in, approximate=False), 2, axis=-1)
    v = layernorm(v, ln_g, ln_b)
    vb = v.reshape(B, T // B_CHUNK, B_CHUNK, B_GROUPS, B_HALF // B_GROUPS)
    vb = jnp.einsum('gpq,bnqge->bnpge', w_s, vb) + b_s.T[None, None, :, :, None]
    return (u * vb.reshape(B, T, B_HALF)) @ w_out + b_out


def setup_inputs(seed: int = 0) -> dict:
    key = jax.random.key(seed)
    ks = jax.random.split(key, 32)

    def nrm(k, shape, scale):
        return jax.random.normal(k, shape, jnp.float32) * scale

    D, DI, E, H = D_MODEL, A_INNER, B_HALF, A_HEADS
    gate_base = jnp.concatenate([jnp.zeros((H,), jnp.float32), jnp.linspace(3.0, 6.0, H, dtype=jnp.float32),
                                 jnp.zeros((H,), jnp.float32), jnp.linspace(3.0, 6.0, H, dtype=jnp.float32)])
    return {
        "x": nrm(ks[0], (BATCH, SEQ, D), 1.0),
        "c": nrm(ks[1], (BATCH, D), 1.0),
        "ctx": nrm(ks[2], (BATCH, CTX_LEN, D), 1.0),
        "c_ctx": nrm(ks[3], (D,), 1.0),
        "w_ada": nrm(ks[4], (DEPTH, D, N_MOD * D), 0.5 * D ** -0.5),
        "b_ada": nrm(ks[5], (DEPTH, N_MOD * D), 0.02),
        "norm_g": 1.0 + nrm(ks[6], (DEPTH, 6, D), 0.05),
        "ffn_w_in": nrm(ks[7], (DEPTH, 2, D, 2 * D_FF), D ** -0.5),
        "ffn_w_out": nrm(ks[8], (DEPTH, 2, D_FF, D), D_FF ** -0.5),
        "a_w_in": nrm(ks[9], (N_A, D, 2 * DI), D ** -0.5),
        "a_conv_w": nrm(ks[10], (N_A, A_CONV, A_CONV, DI), 1.0 / A_CONV),
        "a_conv_b": nrm(ks[11], (N_A, DI), 0.02),
        "a_w_qkv": nrm(ks[12], (N_A, 3, DI // A_QKV_BLOCK, A_QKV_BLOCK, A_QKV_BLOCK), A_QKV_BLOCK ** -0.5),
        "a_w_gates": nrm(ks[13], (N_A, 3 * DI, 4 * H), 0.5 * (3 * DI) ** -0.5),
        "a_b_gates": gate_base[None, :] + nrm(ks[14], (N_A, 4 * H), 0.1),
        "a_norm_g": 1.0 + nrm(ks[15], (N_A, DI), 0.05),
        "a_skip": 1.0 + nrm(ks[16], (N_A, DI), 0.05),
        "a_w_out": nrm(ks[17], (N_A, DI, D), DI ** -0.5),
        "b_w_in": nrm(ks[18], (N_B, D, 2 * E), D ** -0.5),
        "b_b_in": nrm(ks[19], (N_B, 2 * E), 0.02),
        "b_ln_g": 1.0 + nrm(ks[20], (N_B, E), 0.05),
        "b_ln_b": nrm(ks[21], (N_B, E), 0.02),
        "b_w_s": nrm(ks[22], (N_B, B_GROUPS, B_CHUNK, B_CHUNK), B_CHUNK ** -0.5),
        "b_b_s": 1.0 + nrm(ks[23], (N_B, B_GROUPS, B_CHUNK), 0.05),
        "b_w_out": nrm(ks[24], (N_B, E, D), E ** -0.5),
        "b_b_out": nrm(ks[25], (N_B, D), 0.02),
    }


def reference(x, c, ctx, c_ctx, w_ada, b_ada, norm_g, ffn_w_in, ffn_w_out,
              a_w_in, a_conv_w, a_conv_b, a_w_qkv, a_w_gates, a_b_gates, a_norm_g, a_skip, a_w_out,
              b_w_in, b_b_in, b_ln_g, b_ln_b, b_w_s, b_b_s, b_w_out, b_b_out):
    rows = x.shape[1] // GRID_W
    a_layers = [j for j in range(DEPTH) if j % N_MIXERS == 0]
    for i in range(DEPTH):
        kind = i % N_MIXERS
        li = i // N_MIXERS
        need_in = any(j >= i for j in a_layers)
        need_out = any(j > i for j in a_layers)
        mx = adaln(c, w_ada[i], b_ada[i])
        mc = adaln(c_ctx[None], w_ada[i], b_ada[i])
        ng = norm_g[i]
        x = residual(x, swiglu(modulate(x, ng[0], mx[0], mx[1]), ffn_w_in[i, 0], ffn_w_out[i, 0]), ng[1], mx[2], 0.5)
        cn = None
        if need_in:
            ctx = residual(ctx, swiglu(modulate(ctx, ng[0], mc[0], mc[1]), ffn_w_in[i, 0], ffn_w_out[i, 0]), ng[1], mc[2], 0.5)
            cn = modulate(ctx, ng[2], mc[3], mc[4])
        xn = modulate(x, ng[2], mx[3], mx[4])
        if kind == 0:
            y_x, y_c = mlstm_mixer(xn, cn, rows, a_w_in[li], a_conv_w[li], a_conv_b[li], a_w_qkv[li],
                                   a_w_gates[li], a_b_gates[li], a_norm_g[li], a_skip[li], a_w_out[li], need_out)
        else:
            y_x = gmlp(xn, b_w_in[li], b_b_in[li], b_ln_g[li], b_ln_b[li], b_w_s[li], b_b_s[li], b_w_out[li], b_b_out[li])
            y_c = None
            if need_out:
                y_c = gmlp(cn, b_w_in[li], b_b_in[li], b_ln_g[li], b_ln_b[li], b_w_s[li], b_b_s[li], b_w_out[li], b_b_out[li])
        x = residual(x, y_x, ng[3], mx[5], 1.0)
        x = residual(x, swiglu(modulate(x, ng[4], mx[6], mx[7]), ffn_w_in[i, 1], ffn_w_out[i, 1]), ng[5], mx[8], 0.5)
        if need_out:
            ctx = residual(ctx, y_c, ng[3], mc[5], 1.0)
            ctx = residual(ctx, swiglu(modulate(ctx, ng[4], mc[6], mc[7]), ffn_w_in[i, 1], ffn_w_out[i, 1]), ng[5], mc[8], 0.5)
    return x
```

```python
import functools

import jax
import jax.numpy as jnp
from jax import lax
from jax.experimental import pallas as pl
from jax.experimental.pallas import tpu as pltpu

F32 = jnp.float32
BF16 = jnp.bfloat16

EPS = 1e-6
N_MOD = 9
GRID_W = 64
A_HEADS = 4
A_QKV_BLOCK = 4
B_CHUNK = 128
B_GROUPS = 8

ROW_TILE = 512
MIX_TILE = 256
SCAN_CHUNK = 256
FF_CHUNK = 256
QKV_GROUP = 256
GATE_PAD = 128
VMEM_LIMIT = 56 * 1024 * 1024


def _params(n_axes):
    return pltpu.CompilerParams(dimension_semantics=("arbitrary",) * n_axes,
                                vmem_limit_bytes=VMEM_LIMIT)


def _const_spec(shape):
    nd = len(shape)
    return pl.BlockSpec(shape, lambda *_: (0,) * nd, pipeline_mode=pl.Buffered(1))


def _rms(x, g):
    return x * lax.rsqrt(jnp.mean(x * x, axis=-1, keepdims=True) + EPS) * g


def _silu(x):
    return x * jax.nn.sigmoid(x)


def _mod_index(tile_rows, rows_per_batch, n_batch):
    tiles_per_batch = rows_per_batch // tile_rows

    def index(i):
        return jnp.minimum(i // tiles_per_batch, n_batch)

    return index


def _ada_kernel(c_ref, w_ref, b_ref, o_ref):
    c = c_ref[...]
    a = _silu(c).astype(BF16)
    o_ref[...] = jnp.dot(a, w_ref[...].astype(BF16), preferred_element_type=F32) + b_ref[...]


def _ada_table(cond, w_ada, b_ada):
    depth, d, nd = w_ada.shape
    tn = 1024
    return pl.pallas_call(
        _ada_kernel,
        grid=(depth, nd // tn),
        in_specs=[pl.BlockSpec((8, d), lambda l, j: (0, 0)),
                  pl.BlockSpec((None, d, tn), lambda l, j: (l, 0, j)),
                  pl.BlockSpec((None, 1, tn), lambda l, j: (l, 0, j))],
        out_specs=pl.BlockSpec((None, 8, tn), lambda l, j: (l, 0, j)),
        out_shape=jax.ShapeDtypeStruct((depth, 8, nd), F32),
        compiler_params=_params(2),
        name="ada_table",
    )(cond, w_ada, b_ada.reshape(depth, 1, nd))


def _ffn_kernel(x_ref, m_ref, g_ref, win_ref, wout_ref, o_ref, acc_ref, *, mi, gi, dff):
    x = x_ref[...]
    shift, scale, gate = m_ref[mi:mi + 1, :], m_ref[mi + 1:mi + 2, :], m_ref[mi + 2:mi + 3, :]
    xb = (_rms(x, g_ref[gi:gi + 1, :]) * (1.0 + scale) + shift).astype(BF16)
    for c in range(dff // FF_CHUNK):
        lo = c * FF_CHUNK
        hg = jnp.dot(xb, win_ref[:, lo:lo + FF_CHUNK], preferred_element_type=F32)
        hu = jnp.dot(xb, win_ref[:, dff + lo:dff + lo + FF_CHUNK], preferred_element_type=F32)
        a = (_silu(hg) * hu).astype(BF16)
        part = jnp.dot(a, wout_ref[lo:lo + FF_CHUNK, :], preferred_element_type=F32)
        if c == 0:
            acc_ref[...] = part
        else:
            acc_ref[...] += part
    o_ref[...] = x + 0.5 * gate * _rms(acc_ref[...], g_ref[gi + 1:gi + 2, :])


def _ffn(x, mod, ng, w_in, w_out, *, mi, gi, rows_out, rows_per_batch, n_batch):
    d = x.shape[1]
    dff = w_out.shape[0]
    tm = ROW_TILE
    midx = _mod_index(tm, rows_per_batch, n_batch)
    return pl.pallas_call(
        functools.partial(_ffn_kernel, mi=mi, gi=gi, dff=dff),
        grid=(rows_out // tm,),
        in_specs=[pl.BlockSpec((tm, d), lambda i: (i, 0)),
                  pl.BlockSpec((None, N_MOD, d), lambda i: (midx(i), 0, 0)),
                  _const_spec(ng.shape), _const_spec(w_in.shape), _const_spec(w_out.shape)],
        out_specs=pl.BlockSpec((tm, d), lambda i: (i, 0)),
        out_shape=jax.ShapeDtypeStruct((rows_out, d), F32),
        scratch_shapes=[pltpu.VMEM((tm, d), F32)],
        compiler_params=_params(1),
        name="ffn_half_step",
    )(x, mod, ng, w_in, w_out)


def _inproj_kernel(x_ref, m_ref, g_ref, w_ref, xm_ref, z_ref, *, di):
    x = x_ref[...]
    shift, scale = m_ref[3:4, :], m_ref[4:5, :]
    xb = (_rms(x, g_ref[2:3, :]) * (1.0 + scale) + shift).astype(BF16)
    ck = 1024
    for c in range(di // ck):
        xm_ref[:, c * ck:(c + 1) * ck] = jnp.dot(xb, w_ref[:, c * ck:(c + 1) * ck],
                                                 preferred_element_type=F32)
        z_ref[:, c * ck:(c + 1) * ck] = jnp.dot(xb, w_ref[:, di + c * ck:di + (c + 1) * ck],
                                                preferred_element_type=F32)


def _mlstm_inproj(x, mod, ng, w_in, *, rows_per_batch, n_batch):
    r, d = x.shape
    di = w_in.shape[1] // 2
    tm = ROW_TILE
    midx = _mod_index(tm, rows_per_batch, n_batch)
    return pl.pallas_call(
        functools.partial(_inproj_kernel, di=di),
        grid=(r // tm,),
        in_specs=[pl.BlockSpec((tm, d), lambda i: (i, 0)),
                  pl.BlockSpec((None, N_MOD, d), lambda i: (midx(i), 0, 0)),
                  _const_spec(ng.shape), _const_spec(w_in.shape)],
        out_specs=[pl.BlockSpec((tm, di), lambda i: (i, 0)), pl.BlockSpec((tm, di), lambda i: (i, 0))],
        out_shape=[jax.ShapeDtypeStruct((r, di), F32), jax.ShapeDtypeStruct((r, di), F32)],
        compiler_params=_params(1),
        name="mlstm_inproj",
    )(x, mod, ng, w_in)


def _conv_qkv_kernel(xm_ref, top_ref, bot_ref, cw_ref, cb_ref, wq_ref, wk_ref, wkt_ref, wv_ref,
                     wg_ref, bg_ref, bgc_ref,
                     xc_ref, q_ref, k_ref, kt_ref, v_ref, g_ref, gt_ref,
                     ext_ref, gacc_ref, *, tm, x_tiles, tiles_per_batch, ctx_len, qscale):
    i = pl.program_id(0)
    di = xm_ref.shape[1]
    is_ctx = i >= x_tiles
    jloc = i % tiles_per_batch
    top_ok = jnp.logical_and(jnp.logical_not(is_ctx), jloc > 0)
    bot_ok = jnp.logical_and(jnp.logical_not(is_ctx), jloc < tiles_per_batch - 1)
    w = GRID_W
    pad = 8
    base = pad + w
    ext_ref[0:pad, :] = jnp.zeros((pad, di), F32)
    ext_ref[pad:base, :] = top_ref[...] * jnp.where(top_ok, 1.0, 0.0).astype(F32)
    ext_ref[base:base + tm, :] = xm_ref[...]
    ext_ref[base + tm:base + tm + w, :] = bot_ref[...] * jnp.where(bot_ok, 1.0, 0.0).astype(F32)
    ext_ref[base + tm + w:base + tm + w + pad, :] = jnp.zeros((pad, di), F32)

    gq = QKV_GROUP
    r_idx = lax.broadcasted_iota(jnp.int32, (tm, gq), 0)
    col = jnp.where(is_ctx, r_idx & (ctx_len - 1), r_idx & (w - 1))
    last = jnp.where(is_ctx, ctx_len - 1, w - 1)
    left_ok = (col != 0).astype(F32)
    right_ok = (col != last).astype(F32)
    vert = jnp.where(is_ctx, 0.0, 1.0).astype(F32)

    for g in range(di // gq):
        cs = slice(g * gq, (g + 1) * gq)

        def tap(dy, dx):
            o = base + w * dy + dx
            return ext_ref[o:o + tm, cs]

        def wt(ky, kx):
            return cw_ref[3 * ky + kx:3 * ky + kx + 1, cs]

        mid = (wt(1, 1) * tap(0, 0) + vert * (wt(0, 1) * tap(-1, 0) + wt(2, 1) * tap(1, 0)))
        lft = (wt(1, 0) * tap(0, -1) + vert * (wt(0, 0) * tap(-1, -1) + wt(2, 0) * tap(1, -1)))
        rgt = (wt(1, 2) * tap(0, 1) + vert * (wt(0, 2) * tap(-1, 1) + wt(2, 2) * tap(1, 1)))
        conv = mid + left_ok * lft + right_ok * rgt + cb_ref[:, cs]
        xc = _silu(conv)
        xc_ref[:, cs] = xc
        xcb = xc.astype(BF16)
        xmb = xm_ref[:, cs].astype(BF16)
        q = jnp.dot(xcb, wq_ref[g], preferred_element_type=F32)
        k = jnp.dot(xcb, wk_ref[g], preferred_element_type=F32)
        v = jnp.dot(xmb, wv_ref[g], preferred_element_type=F32)
        kt = lax.dot_general(wkt_ref[g], xcb, (((1,), (1,)), ((), ())), preferred_element_type=F32)
        qb, kb, vb = q.astype(BF16), k.astype(BF16), v.astype(BF16)
        q_ref[:, cs] = (q * qscale).astype(BF16)
        k_ref[:, cs] = kb
        v_ref[:, cs] = vb
        kt_ref[cs, :] = kt.astype(BF16)
        part = (jnp.dot(qb, wg_ref[0, cs, :], preferred_element_type=F32)
                + jnp.dot(kb, wg_ref[1, cs, :], preferred_element_type=F32)
                + jnp.dot(vb, wg_ref[2, cs, :], preferred_element_type=F32))
        if g == 0:
            gacc_ref[...] = part
        else:
            gacc_ref[...] += part
    ng = g_ref.shape[1]
    acc = gacc_ref[...]
    g_ref[...] = acc[:, 0:ng] + bg_ref[...]
    gt_ref[...] = acc.T[0:ng, :] + bgc_ref[...]


def _mlstm_conv_qkv(xm, conv_w9, conv_b, wq, wk, wkt, wv, wg, bg, *, rows_per_batch, n_batch, ctx_len):
    r, di = xm.shape
    tm = MIX_TILE
    w = GRID_W
    x_tiles = n_batch * rows_per_batch // tm
    tiles_per_batch = rows_per_batch // tm
    hb = tm // w
    last_halo = r // w - 1
    ng = bg.shape[0]
    dh = di // A_HEADS
    kern = functools.partial(_conv_qkv_kernel, tm=tm, x_tiles=x_tiles, tiles_per_batch=tiles_per_batch,
                             ctx_len=ctx_len, qscale=float(dh) ** -0.5)
    row = pl.BlockSpec((tm, di), lambda i: (i, 0))
    return pl.pallas_call(
        kern,
        grid=(r // tm,),
        in_specs=[row,
                  pl.BlockSpec((w, di), lambda i: (jnp.maximum(i * hb - 1, 0), 0)),
                  pl.BlockSpec((w, di), lambda i: (jnp.minimum((i + 1) * hb, last_halo), 0)),
                  _const_spec(conv_w9.shape), _const_spec((1, di)),
                  _const_spec(wq.shape), _const_spec(wk.shape), _const_spec(wkt.shape),
                  _const_spec(wv.shape), _const_spec(wg.shape),
                  _const_spec((1, ng)), _const_spec((ng, 1))],
        out_specs=[row, row, row, pl.BlockSpec((di, tm), lambda i: (0, i)), row,
                   pl.BlockSpec((tm, ng), lambda i: (i, 0)), pl.BlockSpec((ng, tm), lambda i: (0, i))],
        out_shape=[jax.ShapeDtypeStruct((r, di), F32),
                   jax.ShapeDtypeStruct((r, di), BF16),
                   jax.ShapeDtypeStruct((r, di), BF16),
                   jax.ShapeDtypeStruct((di, r), BF16),
                   jax.ShapeDtypeStruct((r, di), BF16),
                   jax.ShapeDtypeStruct((r, ng), F32),
                   jax.ShapeDtypeStruct((ng, r), F32)],
        scratch_shapes=[pltpu.VMEM((tm + 2 * w + 16, di), F32), pltpu.VMEM((tm, GATE_PAD), F32)],
        compiler_params=_params(1),
        name="mlstm_conv_qkv",
    )(xm, xm, xm, conv_w9, conv_b.reshape(1, di), wq, wk, wkt, wv, wg, bg.reshape(1, ng), bg.reshape(ng, 1))


def _scan_kernel(q_ref, kt_ref, k_ref, v_ref, g_ref, it_ref, ft_ref, h_ref, c_ref, n_ref, m_ref, *, L):
    hd = pl.program_id(1)
    dr = pl.program_id(2)
    j = pl.program_id(3)

    @pl.when(j == 0)
    def _():
        c_ref[...] = jnp.zeros_like(c_ref)
        n_ref[...] = jnp.zeros_like(n_ref)
        m_ref[...] = jnp.zeros_like(m_ref)

    q = q_ref[...]
    v = v_ref[...]
    g = g_ref[...]
    ng = g.shape[1]
    lane = lax.broadcasted_iota(jnp.int32, (L, ng), 1)
    i_col = jnp.sum(jnp.where(lane == dr * 2 * A_HEADS + hd, g, 0.0), axis=1, keepdims=True)
    f_col = jnp.sum(jnp.where(lane == dr * 2 * A_HEADS + A_HEADS + hd, g, 0.0), axis=1, keepdims=True)
    lf_col = jax.nn.log_sigmoid(f_col)
    i_row = it_ref[...]
    lf_row = jax.nn.log_sigmoid(ft_ref[...])

    t_idx = lax.broadcasted_iota(jnp.int32, (L, L), 0)
    s_idx = lax.broadcasted_iota(jnp.int32, (L, L), 1)
    sgn = 1 - 2 * dr
    mask = (s_idx - t_idx) * sgn <= 0
    mask_t = (t_idx - s_idx) * sgn <= 0
    b_col = jnp.sum(jnp.where(mask, lf_row, 0.0), axis=1, keepdims=True)
    b_row = jnp.sum(jnp.where(mask_t, lf_col, 0.0), axis=0, keepdims=True)
    b_tot = jnp.sum(lf_row, axis=1, keepdims=True)

    m_prev = m_ref[0:1, 0:1]
    log_d = jnp.where(mask, b_col - b_row + i_row, -jnp.inf)
    m_inter = b_col + m_prev
    m_t = jnp.maximum(m_inter, jnp.max(log_d, axis=1, keepdims=True))
    dmat = jnp.exp(log_d - m_t)
    s = jnp.dot(q, kt_ref[...], preferred_element_type=F32) * dmat
    scale_inter = jnp.exp(m_inter - m_t)
    c_prev = c_ref[...]
    n_prev = n_ref[...]
    inter = jnp.dot(q, c_prev.astype(BF16), preferred_element_type=F32)
    num = jnp.dot(s.astype(BF16), v, preferred_element_type=F32) + scale_inter * inter
    qf = q.astype(F32)
    den = (jnp.sum(s, axis=1, keepdims=True)
           + scale_inter * jnp.sum(qf * n_prev, axis=1, keepdims=True))
    h_ref[...] = num / jnp.maximum(jnp.abs(den), jnp.exp(-m_t))

    log_w = b_tot - b_col + i_col
    m_new = jnp.maximum(b_tot + m_prev, jnp.max(log_w, axis=0, keepdims=True))
    wgt = jnp.exp(log_w - m_new)
    decay = jnp.exp(b_tot + m_prev - m_new)
    wv = (wgt * v.astype(F32)).astype(BF16)
    c_ref[...] = decay * c_prev + jnp.dot(kt_ref[...], wv, preferred_element_type=F32)
    n_ref[...] = decay * n_prev + jnp.sum(wgt * k_ref[...].astype(F32), axis=0, keepdims=True)
    m_ref[...] = jnp.broadcast_to(m_new, m_ref.shape)


def _mlstm_scan(q, kt, k, v, g, gt3, *, rows_per_batch, n_batch, ctx_len):
    r, di = q.shape
    L = SCAN_CHUNK
    dh = di // A_HEADS
    nctx = ctx_len // L
    nx = rows_per_batch // L
    ctx_base = n_batch * nx

    def rowblk(b, d, j):
        cj = jnp.where(d == 0, j, nctx - 1 - j)
        xj = jnp.where(d == 0, j - nctx, nx - 1 - (j - nctx))
        return jnp.where(j < nctx, ctx_base + b * nctx + cj, b * nx + xj)

    tok = pl.BlockSpec((L, dh), lambda b, h, d, j: (rowblk(b, d, j), h))
    return pl.pallas_call(
        functools.partial(_scan_kernel, L=L),
        grid=(n_batch, A_HEADS, 2, nctx + nx),
        in_specs=[tok,
                  pl.BlockSpec((dh, L), lambda b, h, d, j: (h, rowblk(b, d, j))),
                  tok, tok,
                  pl.BlockSpec((L, g.shape[1]), lambda b, h, d, j: (rowblk(b, d, j), 0)),
                  pl.BlockSpec((None, 1, L), lambda b, h, d, j: (d * 2 * A_HEADS + h, 0, rowblk(b, d, j))),
                  pl.BlockSpec((None, 1, L),
                               lambda b, h, d, j: (d * 2 * A_HEADS + A_HEADS + h, 0, rowblk(b, d, j)))],
        out_specs=pl.BlockSpec((None, L, dh), lambda b, h, d, j: (d, rowblk(b, d, j), h)),
        out_shape=jax.ShapeDtypeStruct((2, r, di), F32),
        scratch_shapes=[pltpu.VMEM((dh, dh), F32), pltpu.VMEM((1, dh), F32), pltpu.VMEM((8, 128), F32)],
        compiler_params=_params(4),
        name="mlstm_scan",
    )(q, kt, k, v, g, gt3, gt3)


def _mlstm_out_kernel(h_ref, xc_ref, z_ref, x_ref, m_ref, g_ref, ag_ref, sk_ref, w_ref, o_ref, acc_ref):
    di = xc_ref.shape[1]
    dh = di // A_HEADS
    for hd in range(A_HEADS):
        cs = slice(hd * dh, (hd + 1) * dh)
        h = h_ref[0, :, cs] + h_ref[1, :, cs]
        mu = jnp.mean(h, axis=-1, keepdims=True)
        hc = h - mu
        var = jnp.mean(hc * hc, axis=-1, keepdims=True)
        hn = hc * lax.rsqrt(var + EPS) * ag_ref[:, cs]
        t = ((hn + sk_ref[:, cs] * xc_ref[:, cs]) * _silu(z_ref[:, cs])).astype(BF16)
        part = jnp.dot(t, w_ref[cs, :], preferred_element_type=F32)
        if hd == 0:
            acc_ref[...] = part
        else:
            acc_ref[...] += part
    o_ref[...] = x_ref[...] + m_ref[5:6, :] * _rms(acc_ref[...], g_ref[3:4, :])


def _mlstm_out(h2, xc, z, x, mod, ng, a_norm_g, a_skip, w_out, *, rows_out, rows_per_batch, n_batch):
    di = xc.shape[1]
    d = x.shape[1]
    tm = MIX_TILE
    midx = _mod_index(tm, rows_per_batch, n_batch)
    row_di = pl.BlockSpec((tm, di), lambda i: (i, 0))
    row_d = pl.BlockSpec((tm, d), lambda i: (i, 0))
    return pl.pallas_call(
        _mlstm_out_kernel,
        grid=(rows_out // tm,),
        in_specs=[pl.BlockSpec((2, tm, di), lambda i: (0, i, 0)), row_di, row_di, row_d,
                  pl.BlockSpec((None, N_MOD, d), lambda i: (midx(i), 0, 0)),
                  _const_spec(ng.shape), _const_spec((1, di)), _const_spec((1, di)), _const_spec(w_out.shape)],
        out_specs=row_d,
        out_shape=jax.ShapeDtypeStruct((rows_out, d), F32),
        scratch_shapes=[pltpu.VMEM((tm, d), F32)],
        compiler_params=_params(1),
        name="mlstm_out",
    )(h2, xc, z, x, mod, ng, a_norm_g.reshape(1, di), a_skip.reshape(1, di), w_out)


def _gelu(x):
    return 0.5 * x * (1.0 + lax.erf(x * (2.0 ** -0.5)))


def _gmlp_kernel(x_ref, m_ref, g_ref, win_ref, bin_ref, lng_ref, lnb_ref, ws_ref, bs_ref, wout_ref,
                 bout_ref, o_ref, u_ref, v_ref, t_ref, *, e):
    x = x_ref[...]
    tm = x.shape[0]
    shift, scale = m_ref[3:4, :], m_ref[4:5, :]
    xb = (_rms(x, g_ref[2:3, :]) * (1.0 + scale) + shift).astype(BF16)
    ck = 512
    for c in range(e // ck):
        cs = slice(c * ck, (c + 1) * ck)
        cs2 = slice(e + c * ck, e + (c + 1) * ck)
        u_ref[:, cs] = _gelu(jnp.dot(xb, win_ref[:, cs], preferred_element_type=F32) + bin_ref[:, cs])
        v_ref[:, cs] = _gelu(jnp.dot(xb, win_ref[:, cs2], preferred_element_type=F32) + bin_ref[:, cs2])
    v = v_ref[...]
    mu = jnp.mean(v, axis=-1, keepdims=True)
    vc = v - mu
    var = jnp.mean(vc * vc, axis=-1, keepdims=True)
    v_ref[...] = vc * lax.rsqrt(var + EPS) * lng_ref[...] + lnb_ref[...]
    ge = e // B_GROUPS
    for rc in range(tm // B_CHUNK):
        rs = slice(rc * B_CHUNK, (rc + 1) * B_CHUNK)
        for g in range(B_GROUPS):
            cs = slice(g * ge, (g + 1) * ge)
            vb = jnp.dot(ws_ref[g], v_ref[rs, cs].astype(BF16), preferred_element_type=F32) + bs_ref[:, g:g + 1]
            t_ref[rs, cs] = (u_ref[rs, cs] * vb).astype(BF16)
    y = jnp.dot(t_ref[...], wout_ref[...], preferred_element_type=F32) + bout_ref[...]
    o_ref[...] = x + m_ref[5:6, :] * _rms(y, g_ref[3:4, :])


def _gmlp(x, mod, ng, w_in, b_in, ln_g, ln_b, w_s, b_s_t, w_out, b_out, *, rows_out, rows_per_batch, n_batch):
    d = x.shape[1]
    e = w_out.shape[0]
    tm = MIX_TILE
    midx = _mod_index(tm, rows_per_batch, n_batch)
    row_d = pl.BlockSpec((tm, d), lambda i: (i, 0))
    return pl.pallas_call(
        functools.partial(_gmlp_kernel, e=e),
        grid=(rows_out // tm,),
        in_specs=[row_d, pl.BlockSpec((None, N_MOD, d), lambda i: (midx(i), 0, 0)),
                  _const_spec(ng.shape), _const_spec(w_in.shape), _const_spec((1, 2 * e)),
                  _const_spec((1, e)), _const_spec((1, e)), _const_spec(w_s.shape), _const_spec(b_s_t.shape),
                  _const_spec(w_out.shape), _const_spec((1, d))],
        out_specs=row_d,
        out_shape=jax.ShapeDtypeStruct((rows_out, d), F32),
        scratch_shapes=[pltpu.VMEM((tm, e), F32), pltpu.VMEM((tm, e), F32), pltpu.VMEM((tm, e), BF16)],
        compiler_params=_params(1),
        name="gmlp_mixer",
    )(x, mod, ng, w_in, b_in.reshape(1, 2 * e), ln_g.reshape(1, e), ln_b.reshape(1, e), w_s, b_s_t,
      w_out, b_out.reshape(1, d))


def _dense_blockdiag(w):
    nb, bo, bi = w.shape
    per = QKV_GROUP // bi
    wg = w.reshape(nb // per, per, bo, bi)
    eye = jnp.eye(per, dtype=w.dtype)
    dense = jnp.einsum('gnoi,nm->gnimo', wg, eye)
    return dense.reshape(nb // per, per * bi, per * bo)


def kernel(x, c, ctx, c_ctx, w_ada, b_ada, norm_g, ffn_w_in, ffn_w_out, a_w_in, a_conv_w, a_conv_b, a_w_qkv, a_w_gates, a_b_gates, a_norm_g, a_skip, a_w_out, b_w_in, b_b_in, b_ln_g, b_ln_b, b_w_s, b_b_s, b_w_out, b_b_out):
    n_batch, seq, d = x.shape
    ctx_len = ctx.shape[1]
    depth = w_ada.shape[0]
    rows_x = n_batch * seq
    rows_all = rows_x + n_batch * ctx_len
    di = a_w_out.shape[1]
    assert seq % ROW_TILE == 0 and (n_batch * ctx_len) % ROW_TILE == 0 and ctx_len % MIX_TILE == 0
    assert seq % GRID_W == 0 and MIX_TILE % GRID_W == 0 and ctx_len % SCAN_CHUNK == 0
    assert ctx_len & (ctx_len - 1) == 0 and a_w_gates.shape[2] == 4 * A_HEADS

    cond = jnp.concatenate([c, c_ctx[None], jnp.zeros((8 - n_batch - 1, d), F32)], axis=0)
    mods = _ada_table(cond, w_ada, b_ada).reshape(depth, 8, N_MOD, d)

    h = jnp.concatenate([x.reshape(rows_x, d), ctx.reshape(n_batch * ctx_len, d)], axis=0)
    common = dict(rows_per_batch=seq, n_batch=n_batch)
    mixer_layers = [l for l in range(depth) if l % 2 == 0]
    for l in range(depth):
        li = l // 2
        need_in = any(j >= l for j in mixer_layers)
        need_out = any(j > l for j in mixer_layers)
        rows_in = rows_all if need_in else rows_x
        rows_out = rows_all if need_out else rows_x
        mod, ng = mods[l], norm_g[l]
        h = _ffn(h, mod, ng, ffn_w_in[l, 0].astype(BF16), ffn_w_out[l, 0].astype(BF16),
                 mi=0, gi=0, rows_out=rows_in, **common)
        if l % 2 == 0:
            xm, z = _mlstm_inproj(h, mod, ng, a_w_in[li].astype(BF16), **common)
            wqkv = a_w_qkv[li]
            wq, wk, wv = (_dense_blockdiag(wqkv[t]).astype(BF16) for t in range(3))
            wkt = jnp.swapaxes(wk, 1, 2)
            wg = jnp.pad(a_w_gates[li].reshape(3, di, -1),
                         ((0, 0), (0, 0), (0, GATE_PAD - a_w_gates.shape[2]))).astype(BF16)
            xc, q, k, kt, v, g, gt = _mlstm_conv_qkv(
                xm, a_conv_w[li].reshape(9, di), a_conv_b[li], wq, wk, wkt, wv, wg, a_b_gates[li],
                ctx_len=ctx_len, **common)
            h2 = _mlstm_scan(q, kt, k, v, g, gt.reshape(gt.shape[0], 1, gt.shape[1]), ctx_len=ctx_len, **common)
            h = _mlstm_out(h2, xc, z, h, mod, ng, a_norm_g[li], a_skip[li], a_w_out[li].astype(BF16),
                           rows_out=rows_out, **common)
        else:
            h = _gmlp(h, mod, ng, b_w_in[li].astype(BF16), b_b_in[li], b_ln_g[li], b_ln_b[li],
                      b_w_s[li].astype(BF16), b_b_s[li].T, b_w_out[li].astype(BF16), b_b_out[li],
                      rows_out=rows_out, **common)
        h = _ffn(h, mod, ng, ffn_w_in[l, 1].astype(BF16), ffn_w_out[l, 1].astype(BF16),
                 mi=6, gi=4, rows_out=rows_out, **common)
    return h.reshape(n_batch, seq, d)
```

```python
import functools

import jax
import jax.numpy as jnp
from jax import lax
from jax.experimental import pallas as pl
from jax.experimental.pallas import tpu as pltpu

F32 = jnp.float32
BF16 = jnp.bfloat16

EPS = 1e-6
N_MOD = 9
GRID_W = 64
A_HEADS = 4
A_QKV_BLOCK = 4
B_CHUNK = 128
B_GROUPS = 8

ROW_TILE = 512
MIX_TILE = 256
SCAN_CHUNK = 256
SCAN_HEADS_PER_STEP = 4
FF_CHUNK = 256
QKV_GROUP = 256
GATE_PAD = 128
VMEM_LIMIT = 56 * 1024 * 1024


def _params(n_axes):
    return pltpu.CompilerParams(dimension_semantics=("arbitrary",) * n_axes,
                                vmem_limit_bytes=VMEM_LIMIT)


def _const_spec(shape):
    nd = len(shape)
    return pl.BlockSpec(shape, lambda *_: (0,) * nd, pipeline_mode=pl.Buffered(1))


def _layer_spec(stack, idx):
    lead = len(idx)
    return pl.BlockSpec((None,) * lead + tuple(stack.shape[lead:]), lambda *_: tuple(idx) + (0, 0),
                        pipeline_mode=pl.Buffered(1))


def _rms(x, g):
    return x * lax.rsqrt(jnp.mean(x * x, axis=-1, keepdims=True) + EPS) * g


def _silu(x):
    return x * jax.nn.sigmoid(x)


def _mod_index(tile_rows, rows_per_batch, n_batch):
    tiles_per_batch = rows_per_batch // tile_rows

    def index(i):
        return jnp.minimum(i // tiles_per_batch, n_batch)

    return index


def _ada_kernel(c_ref, w_ref, b_ref, o_ref):
    c = c_ref[...]
    a = _silu(c).astype(BF16)
    o_ref[...] = jnp.dot(a, w_ref[...].astype(BF16), preferred_element_type=F32) + b_ref[...]


def _ada_table(cond, w_ada, b_ada):
    depth, d, nd = w_ada.shape
    tn = 1024
    return pl.pallas_call(
        _ada_kernel,
        grid=(depth, nd // tn),
        in_specs=[pl.BlockSpec((8, d), lambda l, j: (0, 0)),
                  pl.BlockSpec((None, d, tn), lambda l, j: (l, 0, j)),
                  pl.BlockSpec((None, 1, tn), lambda l, j: (l, 0, j))],
        out_specs=pl.BlockSpec((None, 8, tn), lambda l, j: (l, 0, j)),
        out_shape=jax.ShapeDtypeStruct((depth, 8, nd), F32),
        compiler_params=_params(2),
        name="ada_table",
    )(cond, w_ada, b_ada.reshape(depth, 1, nd))


def _ffn_kernel(x_ref, m_ref, g_ref, win_ref, wout_ref, o_ref, acc_ref, *, mi, gi, dff):
    x = x_ref[...]
    shift, scale, gate = m_ref[mi:mi + 1, :], m_ref[mi + 1:mi + 2, :], m_ref[mi + 2:mi + 3, :]
    xb = (_rms(x, g_ref[gi:gi + 1, :]) * (1.0 + scale) + shift).astype(BF16)
    for c in range(dff // FF_CHUNK):
        lo = c * FF_CHUNK
        hg = jnp.dot(xb, win_ref[:, lo:lo + FF_CHUNK], preferred_element_type=F32)
        hu = jnp.dot(xb, win_ref[:, dff + lo:dff + lo + FF_CHUNK], preferred_element_type=F32)
        a = (_silu(hg) * hu).astype(BF16)
        part = jnp.dot(a, wout_ref[lo:lo + FF_CHUNK, :], preferred_element_type=F32)
        if c == 0:
            acc_ref[...] = part
        else:
            acc_ref[...] += part
    o_ref[...] = x + 0.5 * gate * _rms(acc_ref[...], g_ref[gi + 1:gi + 2, :])


def _ffn(x, mod, ng, w_in, w_out, widx, *, mi, gi, rows_out, rows_per_batch, n_batch):
    d = x.shape[1]
    dff = w_out.shape[-2]
    tm = ROW_TILE
    midx = _mod_index(tm, rows_per_batch, n_batch)
    return pl.pallas_call(
        functools.partial(_ffn_kernel, mi=mi, gi=gi, dff=dff),
        grid=(rows_out // tm,),
        in_specs=[pl.BlockSpec((tm, d), lambda i: (i, 0)),
                  pl.BlockSpec((None, N_MOD, d), lambda i: (midx(i), 0, 0)),
                  _const_spec(ng.shape), _layer_spec(w_in, widx), _layer_spec(w_out, widx)],
        out_specs=pl.BlockSpec((tm, d), lambda i: (i, 0)),
        out_shape=jax.ShapeDtypeStruct((rows_out, d), F32),
        scratch_shapes=[pltpu.VMEM((tm, d), F32)],
        compiler_params=_params(1),
        name="ffn_half_step",
    )(x, mod, ng, w_in, w_out)


def _inproj_kernel(x_ref, m_ref, g_ref, w_ref, xm_ref, z_ref, *, di):
    x = x_ref[...]
    shift, scale = m_ref[3:4, :], m_ref[4:5, :]
    xb = (_rms(x, g_ref[2:3, :]) * (1.0 + scale) + shift).astype(BF16)
    ck = 1024
    for c in range(di // ck):
        xm_ref[:, c * ck:(c + 1) * ck] = jnp.dot(xb, w_ref[:, c * ck:(c + 1) * ck],
                                                 preferred_element_type=F32)
        z_ref[:, c * ck:(c + 1) * ck] = jnp.dot(xb, w_ref[:, di + c * ck:di + (c + 1) * ck],
                                                preferred_element_type=F32)


def _mlstm_inproj(x, mod, ng, w_in, widx, *, rows_per_batch, n_batch):
    r, d = x.shape
    di = w_in.shape[-1] // 2
    tm = ROW_TILE
    midx = _mod_index(tm, rows_per_batch, n_batch)
    return pl.pallas_call(
        functools.partial(_inproj_kernel, di=di),
        grid=(r // tm,),
        in_specs=[pl.BlockSpec((tm, d), lambda i: (i, 0)),
                  pl.BlockSpec((None, N_MOD, d), lambda i: (midx(i), 0, 0)),
                  _const_spec(ng.shape), _layer_spec(w_in, widx)],
        out_specs=[pl.BlockSpec((tm, di), lambda i: (i, 0)), pl.BlockSpec((tm, di), lambda i: (i, 0))],
        out_shape=[jax.ShapeDtypeStruct((r, di), F32), jax.ShapeDtypeStruct((r, di), F32)],
        compiler_params=_params(1),
        name="mlstm_inproj",
    )(x, mod, ng, w_in)


def _conv_qkv_kernel(xm_ref, top_ref, bot_ref, cw_ref, cb_ref, wq_ref, wk_ref, wkt_ref, wv_ref,
                     wg_ref, bg_ref, bgc_ref,
                     xc_ref, q_ref, kt_ref, v_ref, g_ref, gt_ref,
                     ext_ref, gacc_ref, *, tm, x_tiles, tiles_per_batch, ctx_len, qscale):
    i = pl.program_id(0)
    di = xm_ref.shape[1]
    is_ctx = i >= x_tiles
    jloc = i % tiles_per_batch
    top_ok = jnp.logical_and(jnp.logical_not(is_ctx), jloc > 0)
    bot_ok = jnp.logical_and(jnp.logical_not(is_ctx), jloc < tiles_per_batch - 1)
    w = GRID_W
    pad = 8
    base = pad + w
    ext_ref[0:pad, :] = jnp.zeros((pad, di), F32)
    ext_ref[pad:base, :] = top_ref[...] * jnp.where(top_ok, 1.0, 0.0).astype(F32)
    ext_ref[base:base + tm, :] = xm_ref[...]
    ext_ref[base + tm:base + tm + w, :] = bot_ref[...] * jnp.where(bot_ok, 1.0, 0.0).astype(F32)
    ext_ref[base + tm + w:base + tm + w + pad, :] = jnp.zeros((pad, di), F32)

    gq = QKV_GROUP
    r_idx = lax.broadcasted_iota(jnp.int32, (tm, gq), 0)
    col = jnp.where(is_ctx, r_idx & (ctx_len - 1), r_idx & (w - 1))
    last = jnp.where(is_ctx, ctx_len - 1, w - 1)
    left_ok = (col != 0).astype(F32)
    right_ok = (col != last).astype(F32)
    vert = jnp.where(is_ctx, 0.0, 1.0).astype(F32)

    for g in range(di // gq):
        cs = slice(g * gq, (g + 1) * gq)

        def tap(dy, dx):
            o = base + w * dy + dx
            return ext_ref[o:o + tm, cs]

        def wt(ky, kx):
            return cw_ref[3 * ky + kx:3 * ky + kx + 1, cs]

        mid = (wt(1, 1) * tap(0, 0) + vert * (wt(0, 1) * tap(-1, 0) + wt(2, 1) * tap(1, 0)))
        lft = (wt(1, 0) * tap(0, -1) + vert * (wt(0, 0) * tap(-1, -1) + wt(2, 0) * tap(1, -1)))
        rgt = (wt(1, 2) * tap(0, 1) + vert * (wt(0, 2) * tap(-1, 1) + wt(2, 2) * tap(1, 1)))
        conv = mid + left_ok * lft + right_ok * rgt + cb_ref[:, cs]
        xc = _silu(conv)
        xc_ref[:, cs] = xc
        xcb = xc.astype(BF16)
        xmb = xm_ref[:, cs].astype(BF16)
        q = jnp.dot(xcb, wq_ref[g], preferred_element_type=F32)
        k = jnp.dot(xcb, wk_ref[g], preferred_element_type=F32)
        v = jnp.dot(xmb, wv_ref[g], preferred_element_type=F32)
        kt = lax.dot_general(wkt_ref[g], xcb, (((1,), (1,)), ((), ())), preferred_element_type=F32)
        qb, kb, vb = q.astype(BF16), k.astype(BF16), v.astype(BF16)
        q_ref[:, cs] = (q * qscale).astype(BF16)
        v_ref[:, cs] = vb
        kt_ref[cs, :] = kt.astype(BF16)
        part = (jnp.dot(qb, wg_ref[0, cs, :], preferred_element_type=F32)
                + jnp.dot(kb, wg_ref[1, cs, :], preferred_element_type=F32)
                + jnp.dot(vb, wg_ref[2, cs, :], preferred_element_type=F32))
        if g == 0:
            gacc_ref[...] = part
        else:
            gacc_ref[...] += part
    ng = g_ref.shape[1]
    acc = gacc_ref[...]
    g_ref[...] = acc[:, 0:ng] + bg_ref[...]
    gt_ref[...] = acc.T[0:ng, :] + bgc_ref[...]


def _mlstm_conv_qkv(xm, conv_w9, conv_b, wq, wk, wkt, wv, wg, bg, *, rows_per_batch, n_batch, ctx_len):
    r, di = xm.shape
    tm = MIX_TILE
    w = GRID_W
    x_tiles = n_batch * rows_per_batch // tm
    tiles_per_batch = rows_per_batch // tm
    hb = tm // w
    last_halo = r // w - 1
    ng = bg.shape[0]
    dh = di // A_HEADS
    kern = functools.partial(_conv_qkv_kernel, tm=tm, x_tiles=x_tiles, tiles_per_batch=tiles_per_batch,
                             ctx_len=ctx_len, qscale=float(dh) ** -0.5)
    row = pl.BlockSpec((tm, di), lambda i: (i, 0))
    return pl.pallas_call(
        kern,
        grid=(r // tm,),
        in_specs=[row,
                  pl.BlockSpec((w, di), lambda i: (jnp.maximum(i * hb - 1, 0), 0)),
                  pl.BlockSpec((w, di), lambda i: (jnp.minimum((i + 1) * hb, last_halo), 0)),
                  _const_spec(conv_w9.shape), _const_spec((1, di)),
                  _const_spec(wq.shape), _const_spec(wk.shape), _const_spec(wkt.shape),
                  _const_spec(wv.shape), _const_spec(wg.shape),
                  _const_spec((1, ng)), _const_spec((ng, 1))],
        out_specs=[row, row, pl.BlockSpec((di, tm), lambda i: (0, i)), row,
                   pl.BlockSpec((tm, ng), lambda i: (i, 0)), pl.BlockSpec((ng, tm), lambda i: (0, i))],
        out_shape=[jax.ShapeDtypeStruct((r, di), F32),
                   jax.ShapeDtypeStruct((r, di), BF16),
                   jax.ShapeDtypeStruct((di, r), BF16),
                   jax.ShapeDtypeStruct((r, di), BF16),
                   jax.ShapeDtypeStruct((r, ng), F32),
                   jax.ShapeDtypeStruct((ng, r), F32)],
        scratch_shapes=[pltpu.VMEM((tm + 2 * w + 16, di), F32), pltpu.VMEM((tm, GATE_PAD), F32)],
        compiler_params=_params(1),
        name="mlstm_conv_qkv",
    )(xm, xm, xm, conv_w9, conv_b.reshape(1, di), wq, wk, wkt, wv, wg, bg.reshape(1, ng), bg.reshape(ng, 1))


def _scan_chain(q, kt, v, i_col, f_col, i_row, f_row, vis, vist, neg, c_ref, cb_ref, n_ref, m_ref):
    L = q.shape[0]
    lf_col = jax.nn.log_sigmoid(f_col)
    lf_row = jax.nn.log_sigmoid(f_row)
    b_col = jnp.sum(vis * lf_row, axis=1, keepdims=True)
    b_row = jnp.sum(vist * lf_col, axis=0, keepdims=True)
    b_tot = jnp.sum(lf_row, axis=1, keepdims=True)

    m_prev = m_ref[0:1, 0:1]
    log_d = (b_col - b_row + i_row) + neg
    m_inter = b_col + m_prev
    m_t = jnp.maximum(m_inter, jnp.max(log_d, axis=1, keepdims=True))
    dmat = jnp.exp(log_d - m_t)
    s = jnp.dot(q, kt, preferred_element_type=F32) * dmat
    scale_inter = jnp.exp(m_inter - m_t)
    inter = jnp.dot(q, cb_ref[...], preferred_element_type=F32)
    qn = jnp.dot(q, n_ref[...].astype(BF16), preferred_element_type=F32)[:, 0:1]
    num = jnp.dot(s.astype(BF16), v, preferred_element_type=F32) + scale_inter * inter
    den = jnp.sum(s, axis=1, keepdims=True) + scale_inter * qn
    h = num / jnp.maximum(jnp.abs(den), jnp.exp(-m_t))

    log_w = b_tot - b_col + i_col
    m_new = jnp.maximum(b_tot + m_prev, jnp.max(log_w, axis=0, keepdims=True))
    wgt = jnp.exp(log_w - m_new)
    decay = jnp.exp(b_tot + m_prev - m_new)
    wv = (wgt * v.astype(F32)).astype(BF16)
    c_new = decay * c_ref[...] + jnp.dot(kt, wv, preferred_element_type=F32)
    c_ref[...] = c_new
    cb_ref[...] = c_new.astype(BF16)
    wb = jnp.broadcast_to(wgt, (L, n_ref.shape[1])).astype(BF16)
    n_ref[...] = decay * n_ref[...] + jnp.dot(kt, wb, preferred_element_type=F32)
    m_ref[...] = jnp.broadcast_to(m_new, m_ref.shape)
    return h


def _scan_kernel(qf_ref, ktf_ref, vf_ref, gf_ref, gtf_ref, qb_ref, ktb_ref, vb_ref, gb_ref, gtb_ref,
                 vis_ref, neg_ref, hf_ref, hb_ref, c_ref, cb_ref, n_ref, m_ref, *, hp):
    hg = pl.program_id(1)
    j = pl.program_id(2)

    @pl.when(j == 0)
    def _():
        c_ref[...] = jnp.zeros_like(c_ref)
        cb_ref[...] = jnp.zeros_like(cb_ref)
        n_ref[...] = jnp.zeros_like(n_ref)
        m_ref[...] = jnp.zeros_like(m_ref)

    L = qf_ref.shape[0]
    dh = qf_ref.shape[1] // hp
    streams = ((qf_ref, ktf_ref, vf_ref, gf_ref, gtf_ref, hf_ref), (qb_ref, ktb_ref, vb_ref, gb_ref, gtb_ref, hb_ref))
    for dr, (q_ref, kt_ref, v_ref, g_ref, gt_ref, h_ref) in enumerate(streams):
        g = g_ref[...]
        lane = lax.broadcasted_iota(jnp.int32, g.shape, 1)
        for hh in range(hp):
            head = hg * hp + hh
            i_idx = dr * 2 * A_HEADS + head
            f_idx = i_idx + A_HEADS
            i_col = jnp.sum(jnp.where(lane == i_idx, g, 0.0), axis=1, keepdims=True)
            f_col = jnp.sum(jnp.where(lane == f_idx, g, 0.0), axis=1, keepdims=True)
            i_row = gt_ref[pl.ds(i_idx, 1), :]
            f_row = gt_ref[pl.ds(f_idx, 1), :]
            cs = slice(hh * dh, (hh + 1) * dh)
            h_ref[:, cs] = _scan_chain(
                q_ref[:, cs], kt_ref[cs, :], v_ref[:, cs], i_col, f_col, i_row, f_row,
                vis_ref[dr], vis_ref[1 - dr], neg_ref[dr],
                c_ref.at[dr, hh], cb_ref.at[dr, hh], n_ref.at[dr, hh], m_ref.at[dr, hh])


def _mlstm_scan(q, kt, v, g, gt, *, rows_per_batch, n_batch, ctx_len):
    r, di = q.shape
    L = SCAN_CHUNK
    hp = SCAN_HEADS_PER_STEP
    dh = di // A_HEADS
    nctx = ctx_len // L
    nx = rows_per_batch // L
    ctx_base = n_batch * nx
    ngate = g.shape[1]

    def blk_f(b, j):
        return jnp.where(j < nctx, ctx_base + b * nctx + j, b * nx + (j - nctx))

    def blk_b(b, j):
        return jnp.where(j < nctx, ctx_base + b * nctx + (nctx - 1 - j), b * nx + (nx - 1 - (j - nctx)))

    def stream_specs(blk):
        tok = pl.BlockSpec((L, hp * dh), lambda b, h, j: (blk(b, j), h))
        return [tok, pl.BlockSpec((hp * dh, L), lambda b, h, j: (h, blk(b, j))), tok,
                pl.BlockSpec((L, ngate), lambda b, h, j: (blk(b, j), 0)),
                pl.BlockSpec((ngate, L), lambda b, h, j: (0, blk(b, j)))]

    tri = jnp.tril(jnp.ones((L, L), F32))
    vis = jnp.stack([tri, tri.T])
    neg = jnp.where(vis > 0, 0.0, -jnp.inf).astype(F32)
    out_f = pl.BlockSpec((L, hp * dh), lambda b, h, j: (blk_f(b, j), h))
    out_b = pl.BlockSpec((L, hp * dh), lambda b, h, j: (blk_b(b, j), h))
    return pl.pallas_call(
        functools.partial(_scan_kernel, hp=hp),
        grid=(n_batch, A_HEADS // hp, nctx + nx),
        in_specs=stream_specs(blk_f) + stream_specs(blk_b) + [_const_spec(vis.shape), _const_spec(neg.shape)],
        out_specs=[out_f, out_b],
        out_shape=[jax.ShapeDtypeStruct((r, di), F32), jax.ShapeDtypeStruct((r, di), F32)],
        scratch_shapes=[pltpu.VMEM((2, hp, dh, dh), F32), pltpu.VMEM((2, hp, dh, dh), BF16),
                        pltpu.VMEM((2, hp, dh, GATE_PAD), F32), pltpu.VMEM((2, hp, 8, 128), F32)],
        compiler_params=_params(3),
        name="mlstm_scan",
    )(q, kt, v, g, gt, q, kt, v, g, gt, vis, neg)


def _mlstm_out_kernel(hf_ref, hb_ref, xc_ref, z_ref, x_ref, m_ref, g_ref, ag_ref, sk_ref, w_ref, o_ref, acc_ref):
    di = xc_ref.shape[1]
    dh = di // A_HEADS
    for hd in range(A_HEADS):
        cs = slice(hd * dh, (hd + 1) * dh)
        h = hf_ref[:, cs] + hb_ref[:, cs]
        mu = jnp.mean(h, axis=-1, keepdims=True)
        hc = h - mu
        var = jnp.mean(hc * hc, axis=-1, keepdims=True)
        hn = hc * lax.rsqrt(var + EPS) * ag_ref[:, cs]
        t = ((hn + sk_ref[:, cs] * xc_ref[:, cs]) * _silu(z_ref[:, cs])).astype(BF16)
        part = jnp.dot(t, w_ref[cs, :], preferred_element_type=F32)
        if hd == 0:
            acc_ref[...] = part
        else:
            acc_ref[...] += part
    o_ref[...] = x_ref[...] + m_ref[5:6, :] * _rms(acc_ref[...], g_ref[3:4, :])


def _mlstm_out(hf, hb, xc, z, x, mod, ng, a_norm_g, a_skip, w_out, widx, *, rows_out, rows_per_batch, n_batch):
    di = xc.shape[1]
    d = x.shape[1]
    tm = MIX_TILE
    midx = _mod_index(tm, rows_per_batch, n_batch)
    row_di = pl.BlockSpec((tm, di), lambda i: (i, 0))
    row_d = pl.BlockSpec((tm, d), lambda i: (i, 0))
    return pl.pallas_call(
        _mlstm_out_kernel,
        grid=(rows_out // tm,),
        in_specs=[row_di, row_di, row_di, row_di, row_d,
                  pl.BlockSpec((None, N_MOD, d), lambda i: (midx(i), 0, 0)),
                  _const_spec(ng.shape), _const_spec((1, di)), _const_spec((1, di)), _layer_spec(w_out, widx)],
        out_specs=row_d,
        out_shape=jax.ShapeDtypeStruct((rows_out, d), F32),
        scratch_shapes=[pltpu.VMEM((tm, d), F32)],
        compiler_params=_params(1),
        name="mlstm_out",
    )(hf, hb, xc, z, x, mod, ng, a_norm_g.reshape(1, di), a_skip.reshape(1, di), w_out)


def _gelu(x):
    return 0.5 * x * (1.0 + lax.erf(x * (2.0 ** -0.5)))


def _gmlp_kernel(x_ref, m_ref, g_ref, win_ref, bin_ref, lng_ref, lnb_ref, ws_ref, bs_ref, wout_ref,
                 bout_ref, o_ref, u_ref, v_ref, t_ref, *, e):
    x = x_ref[...]
    tm = x.shape[0]
    shift, scale = m_ref[3:4, :], m_ref[4:5, :]
    xb = (_rms(x, g_ref[2:3, :]) * (1.0 + scale) + shift).astype(BF16)
    ck = 512
    for c in range(e // ck):
        cs = slice(c * ck, (c + 1) * ck)
        cs2 = slice(e + c * ck, e + (c + 1) * ck)
        u_ref[:, cs] = _gelu(jnp.dot(xb, win_ref[:, cs], preferred_element_type=F32) + bin_ref[:, cs])
        v_ref[:, cs] = _gelu(jnp.dot(xb, win_ref[:, cs2], preferred_element_type=F32) + bin_ref[:, cs2])
    v = v_ref[...]
    mu = jnp.mean(v, axis=-1, keepdims=True)
    vc = v - mu
    var = jnp.mean(vc * vc, axis=-1, keepdims=True)
    v_ref[...] = vc * lax.rsqrt(var + EPS) * lng_ref[...] + lnb_ref[...]
    ge = e // B_GROUPS
    for rc in range(tm // B_CHUNK):
        rs = slice(rc * B_CHUNK, (rc + 1) * B_CHUNK)
        for g in range(B_GROUPS):
            cs = slice(g * ge, (g + 1) * ge)
            vb = jnp.dot(ws_ref[g], v_ref[rs, cs].astype(BF16), preferred_element_type=F32) + bs_ref[:, g:g + 1]
            t_ref[rs, cs] = (u_ref[rs, cs] * vb).astype(BF16)
    y = jnp.dot(t_ref[...], wout_ref[...], preferred_element_type=F32) + bout_ref[...]
    o_ref[...] = x + m_ref[5:6, :] * _rms(y, g_ref[3:4, :])


def _gmlp(x, mod, ng, w_in, b_in, ln_g, ln_b, w_s, b_s_t, w_out, b_out, widx, *, rows_out, rows_per_batch,
          n_batch):
    d = x.shape[1]
    e = w_out.shape[-2]
    tm = MIX_TILE
    midx = _mod_index(tm, rows_per_batch, n_batch)
    row_d = pl.BlockSpec((tm, d), lambda i: (i, 0))
    return pl.pallas_call(
        functools.partial(_gmlp_kernel, e=e),
        grid=(rows_out // tm,),
        in_specs=[row_d, pl.BlockSpec((None, N_MOD, d), lambda i: (midx(i), 0, 0)),
                  _const_spec(ng.shape), _layer_spec(w_in, widx), _const_spec((1, 2 * e)),
                  _const_spec((1, e)), _const_spec((1, e)), _const_spec(w_s.shape), _const_spec(b_s_t.shape),
                  _layer_spec(w_out, widx), _const_spec((1, d))],
        out_specs=row_d,
        out_shape=jax.ShapeDtypeStruct((rows_out, d), F32),
        scratch_shapes=[pltpu.VMEM((tm, e), F32), pltpu.VMEM((tm, e), F32), pltpu.VMEM((tm, e), BF16)],
        compiler_params=_params(1),
        name="gmlp_mixer",
    )(x, mod, ng, w_in, b_in.reshape(1, 2 * e), ln_g.reshape(1, e), ln_b.reshape(1, e), w_s, b_s_t,
      w_out, b_out.reshape(1, d))


def _dense_blockdiag(w):
    nb, bo, bi = w.shape
    per = QKV_GROUP // bi
    wg = w.reshape(nb // per, per, bo, bi)
    eye = jnp.eye(per, dtype=w.dtype)
    dense = jnp.einsum('gnoi,nm->gnimo', wg, eye)
    return dense.reshape(nb // per, per * bi, per * bo)


def kernel(x, c, ctx, c_ctx, w_ada, b_ada, norm_g, ffn_w_in, ffn_w_out, a_w_in, a_conv_w, a_conv_b, a_w_qkv, a_w_gates, a_b_gates, a_norm_g, a_skip, a_w_out, b_w_in, b_b_in, b_ln_g, b_ln_b, b_w_s, b_b_s, b_w_out, b_b_out):
    n_batch, seq, d = x.shape
    ctx_len = ctx.shape[1]
    depth = w_ada.shape[0]
    rows_x = n_batch * seq
    rows_all = rows_x + n_batch * ctx_len
    di = a_w_out.shape[1]
    assert seq % ROW_TILE == 0 and (n_batch * ctx_len) % ROW_TILE == 0 and ctx_len % MIX_TILE == 0
    assert seq % GRID_W == 0 and MIX_TILE % GRID_W == 0 and ctx_len % SCAN_CHUNK == 0
    assert ctx_len & (ctx_len - 1) == 0 and a_w_gates.shape[2] == 4 * A_HEADS

    cond = jnp.concatenate([c, c_ctx[None], jnp.zeros((8 - n_batch - 1, d), F32)], axis=0)
    mods = _ada_table(cond, w_ada, b_ada).reshape(depth, 8, N_MOD, d)

    h = jnp.concatenate([x.reshape(rows_x, d), ctx.reshape(n_batch * ctx_len, d)], axis=0)
    common = dict(rows_per_batch=seq, n_batch=n_batch)
    ffn_w_in, ffn_w_out = ffn_w_in.astype(BF16), ffn_w_out.astype(BF16)
    a_w_in, a_w_out = a_w_in.astype(BF16), a_w_out.astype(BF16)
    b_w_in, b_w_out = b_w_in.astype(BF16), b_w_out.astype(BF16)
    mixer_layers = [l for l in range(depth) if l % 2 == 0]
    for l in range(depth):
        li = l // 2
        need_in = any(j >= l for j in mixer_layers)
        need_out = any(j > l for j in mixer_layers)
        rows_in = rows_all if need_in else rows_x
        rows_out = rows_all if need_out else rows_x
        mod, ng = mods[l], norm_g[l]
        h = _ffn(h, mod, ng, ffn_w_in, ffn_w_out, (l, 0),
                 mi=0, gi=0, rows_out=rows_in, **common)
        if l % 2 == 0:
            xm, z = _mlstm_inproj(h, mod, ng, a_w_in, (li,), **common)
            wqkv = a_w_qkv[li]
            wq, wk, wv = (_dense_blockdiag(wqkv[t]).astype(BF16) for t in range(3))
            wkt = jnp.swapaxes(wk, 1, 2)
            wg = jnp.pad(a_w_gates[li].reshape(3, di, -1),
                         ((0, 0), (0, 0), (0, GATE_PAD - a_w_gates.shape[2]))).astype(BF16)
            xc, q, kt, v, g, gt = _mlstm_conv_qkv(
                xm, a_conv_w[li].reshape(9, di), a_conv_b[li], wq, wk, wkt, wv, wg, a_b_gates[li],
                ctx_len=ctx_len, **common)
            hf, hb = _mlstm_scan(q, kt, v, g, gt, ctx_len=ctx_len, **common)
            h = _mlstm_out(hf, hb, xc, z, h, mod, ng, a_norm_g[li], a_skip[li], a_w_out, (li,),
                           rows_out=rows_out, **common)
        else:
            h = _gmlp(h, mod, ng, b_w_in, b_b_in[li], b_ln_g[li], b_ln_b[li],
                      b_w_s[li].astype(BF16), b_b_s[li].T, b_w_out, b_b_out[li], (li,),
                      rows_out=rows_out, **common)
        h = _ffn(h, mod, ng, ffn_w_in, ffn_w_out, (l, 1),
                 mi=6, gi=4, rows_out=rows_out, **common)
    return h.reshape(n_batch, seq, d)
```

```python
import functools

import jax
import jax.numpy as jnp
from jax import lax
from jax.experimental import pallas as pl
from jax.experimental.pallas import tpu as pltpu

F32 = jnp.float32
BF16 = jnp.bfloat16

EPS = 1e-6
N_MOD = 9
GRID_W = 64
A_HEADS = 4
A_QKV_BLOCK = 4
B_CHUNK = 128
B_GROUPS = 8

ROW_TILE = 512
MIX_TILE = 256
GMLP_TILE = 512
GMLP_SUB = 256
SCAN_CHUNK = 256
SCAN_HEADS_PER_STEP = 4
FF_CHUNK = 256
QKV_GROUP = 256
GATE_PAD = 128
VMEM_LIMIT = 56 * 1024 * 1024


def _params(n_axes):
    return pltpu.CompilerParams(dimension_semantics=("arbitrary",) * n_axes,
                                vmem_limit_bytes=VMEM_LIMIT)


def _const_spec(shape):
    nd = len(shape)
    return pl.BlockSpec(shape, lambda *_: (0,) * nd, pipeline_mode=pl.Buffered(1))


def _layer_spec(stack, idx):
    lead = len(idx)
    return pl.BlockSpec((None,) * lead + tuple(stack.shape[lead:]), lambda *_: tuple(idx) + (0, 0),
                        pipeline_mode=pl.Buffered(1))


def _rms(x, g):
    return x * lax.rsqrt(jnp.mean(x * x, axis=-1, keepdims=True) + EPS) * g


def _silu(x):
    return x * jax.nn.sigmoid(x)


def _mod_index(tile_rows, rows_per_batch, n_batch):
    tiles_per_batch = rows_per_batch // tile_rows

    def index(i):
        return jnp.minimum(i // tiles_per_batch, n_batch)

    return index


def _ada_kernel(c_ref, w_ref, b_ref, o_ref):
    c = c_ref[...]
    a = _silu(c).astype(BF16)
    o_ref[...] = jnp.dot(a, w_ref[...].astype(BF16), preferred_element_type=F32) + b_ref[...]


def _ada_table(cond, w_ada, b_ada):
    depth, d, nd = w_ada.shape
    tn = 1024
    return pl.pallas_call(
        _ada_kernel,
        grid=(depth, nd // tn),
        in_specs=[pl.BlockSpec((8, d), lambda l, j: (0, 0)),
                  pl.BlockSpec((None, d, tn), lambda l, j: (l, 0, j)),
                  pl.BlockSpec((None, 1, tn), lambda l, j: (l, 0, j))],
        out_specs=pl.BlockSpec((None, 8, tn), lambda l, j: (l, 0, j)),
        out_shape=jax.ShapeDtypeStruct((depth, 8, nd), F32),
        compiler_params=_params(2),
        name="ada_table",
    )(cond, w_ada, b_ada.reshape(depth, 1, nd))


def _ffn_kernel(x_ref, m_ref, g_ref, win_ref, wout_ref, o_ref, acc_ref, *, mi, gi, dff):
    x = x_ref[...]
    shift, scale, gate = m_ref[mi:mi + 1, :], m_ref[mi + 1:mi + 2, :], m_ref[mi + 2:mi + 3, :]
    xb = (_rms(x, g_ref[gi:gi + 1, :]) * (1.0 + scale) + shift).astype(BF16)
    for c in range(dff // FF_CHUNK):
        lo = c * FF_CHUNK
        hg = jnp.dot(xb, win_ref[:, lo:lo + FF_CHUNK], preferred_element_type=F32)
        hu = jnp.dot(xb, win_ref[:, dff + lo:dff + lo + FF_CHUNK], preferred_element_type=F32)
        a = (_silu(hg) * hu).astype(BF16)
        part = jnp.dot(a, wout_ref[lo:lo + FF_CHUNK, :], preferred_element_type=F32)
        if c == 0:
            acc_ref[...] = part
        else:
            acc_ref[...] += part
    o_ref[...] = x + 0.5 * gate * _rms(acc_ref[...], g_ref[gi + 1:gi + 2, :])


def _ffn(x, mod, ng, w_in, w_out, widx, *, mi, gi, rows_out, rows_per_batch, n_batch):
    d = x.shape[1]
    dff = w_out.shape[-2]
    tm = ROW_TILE
    midx = _mod_index(tm, rows_per_batch, n_batch)
    return pl.pallas_call(
        functools.partial(_ffn_kernel, mi=mi, gi=gi, dff=dff),
        grid=(rows_out // tm,),
        in_specs=[pl.BlockSpec((tm, d), lambda i: (i, 0)),
                  pl.BlockSpec((None, N_MOD, d), lambda i: (midx(i), 0, 0)),
                  _const_spec(ng.shape), _layer_spec(w_in, widx), _layer_spec(w_out, widx)],
        out_specs=pl.BlockSpec((tm, d), lambda i: (i, 0)),
        out_shape=jax.ShapeDtypeStruct((rows_out, d), F32),
        scratch_shapes=[pltpu.VMEM((tm, d), F32)],
        compiler_params=_params(1),
        name="ffn_half_step",
    )(x, mod, ng, w_in, w_out)


def _inproj_kernel(x_ref, m_ref, g_ref, w_ref, xm_ref, z_ref, *, di):
    x = x_ref[...]
    shift, scale = m_ref[3:4, :], m_ref[4:5, :]
    xb = (_rms(x, g_ref[2:3, :]) * (1.0 + scale) + shift).astype(BF16)
    ck = 1024
    for c in range(di // ck):
        xm_ref[:, c * ck:(c + 1) * ck] = jnp.dot(xb, w_ref[:, c * ck:(c + 1) * ck],
                                                 preferred_element_type=F32).astype(BF16)
        z_ref[:, c * ck:(c + 1) * ck] = jnp.dot(xb, w_ref[:, di + c * ck:di + (c + 1) * ck],
                                                preferred_element_type=F32).astype(BF16)


def _mlstm_inproj(x, mod, ng, w_in, widx, *, rows_per_batch, n_batch):
    r, d = x.shape
    di = w_in.shape[-1] // 2
    tm = ROW_TILE
    midx = _mod_index(tm, rows_per_batch, n_batch)
    return pl.pallas_call(
        functools.partial(_inproj_kernel, di=di),
        grid=(r // tm,),
        in_specs=[pl.BlockSpec((tm, d), lambda i: (i, 0)),
                  pl.BlockSpec((None, N_MOD, d), lambda i: (midx(i), 0, 0)),
                  _const_spec(ng.shape), _layer_spec(w_in, widx)],
        out_specs=[pl.BlockSpec((tm, di), lambda i: (i, 0)), pl.BlockSpec((tm, di), lambda i: (i, 0))],
        out_shape=[jax.ShapeDtypeStruct((r, di), BF16), jax.ShapeDtypeStruct((r, di), BF16)],
        compiler_params=_params(1),
        name="mlstm_inproj",
    )(x, mod, ng, w_in)


def _conv_qkv_kernel(xm_ref, top_ref, bot_ref, cw_ref, cb_ref, wq_ref, wk_ref, wkt_ref, wv_ref,
                     wg_ref, bg_ref, bgc_ref,
                     xc_ref, q_ref, kt_ref, v_ref, g_ref, gt_ref,
                     ext_ref, gacc_ref, *, tm, x_tiles, tiles_per_batch, ctx_len, qscale):
    i = pl.program_id(0)
    di = xm_ref.shape[1]
    is_ctx = i >= x_tiles
    jloc = i % tiles_per_batch
    top_ok = jnp.logical_and(jnp.logical_not(is_ctx), jloc > 0)
    bot_ok = jnp.logical_and(jnp.logical_not(is_ctx), jloc < tiles_per_batch - 1)
    w = GRID_W
    pad = 8
    base = pad + w
    ext_ref[0:pad, :] = jnp.zeros((pad, di), F32)
    ext_ref[pad:base, :] = top_ref[...].astype(F32) * jnp.where(top_ok, 1.0, 0.0).astype(F32)
    ext_ref[base:base + tm, :] = xm_ref[...].astype(F32)
    ext_ref[base + tm:base + tm + w, :] = bot_ref[...].astype(F32) * jnp.where(bot_ok, 1.0, 0.0).astype(F32)
    ext_ref[base + tm + w:base + tm + w + pad, :] = jnp.zeros((pad, di), F32)

    gq = QKV_GROUP
    r_idx = lax.broadcasted_iota(jnp.int32, (tm, gq), 0)
    col = jnp.where(is_ctx, r_idx & (ctx_len - 1), r_idx & (w - 1))
    last = jnp.where(is_ctx, ctx_len - 1, w - 1)
    left_ok = (col != 0).astype(F32)
    right_ok = (col != last).astype(F32)
    vert = jnp.where(is_ctx, 0.0, 1.0).astype(F32)

    for g in range(di // gq):
        cs = slice(g * gq, (g + 1) * gq)

        def tap(dy, dx):
            o = base + w * dy + dx
            return ext_ref[o:o + tm, cs]

        def wt(ky, kx):
            row = cw_ref[3 * ky + kx:3 * ky + kx + 1, cs]
            return row if ky == 1 else vert * row

        mid = wt(1, 1) * tap(0, 0) + wt(0, 1) * tap(-1, 0) + wt(2, 1) * tap(1, 0)
        lft = wt(1, 0) * tap(0, -1) + wt(0, 0) * tap(-1, -1) + wt(2, 0) * tap(1, -1)
        rgt = wt(1, 2) * tap(0, 1) + wt(0, 2) * tap(-1, 1) + wt(2, 2) * tap(1, 1)
        conv = mid + left_ok * lft + right_ok * rgt + cb_ref[:, cs]
        xc = _silu(conv)
        xcb = xc.astype(BF16)
        xc_ref[:, cs] = xcb
        xmb = xm_ref[:, cs]
        q = jnp.dot(xcb, wq_ref[g], preferred_element_type=F32)
        k = jnp.dot(xcb, wk_ref[g], preferred_element_type=F32)
        v = jnp.dot(xmb, wv_ref[g], preferred_element_type=F32)
        kt = lax.dot_general(wkt_ref[g], xcb, (((1,), (1,)), ((), ())), preferred_element_type=F32)
        qb, kb, vb = q.astype(BF16), k.astype(BF16), v.astype(BF16)
        q_ref[:, cs] = (q * qscale).astype(BF16)
        v_ref[:, cs] = vb
        kt_ref[cs, :] = kt.astype(BF16)
        part = (jnp.dot(qb, wg_ref[0, cs, :], preferred_element_type=F32)
                + jnp.dot(kb, wg_ref[1, cs, :], preferred_element_type=F32)
                + jnp.dot(vb, wg_ref[2, cs, :], preferred_element_type=F32))
        if g == 0:
            gacc_ref[...] = part
        else:
            gacc_ref[...] += part
    ng = g_ref.shape[1]
    acc = gacc_ref[...]
    g_ref[...] = acc[:, 0:ng] + bg_ref[...]
    gt_ref[...] = acc.T[0:ng, :] + bgc_ref[...]


def _mlstm_conv_qkv(xm, conv_w9, conv_b, wq, wk, wkt, wv, wg, bg, *, rows_per_batch, n_batch, ctx_len):
    r, di = xm.shape
    tm = MIX_TILE
    w = GRID_W
    x_tiles = n_batch * rows_per_batch // tm
    tiles_per_batch = rows_per_batch // tm
    hb = tm // w
    last_halo = r // w - 1
    ng = bg.shape[0]
    dh = di // A_HEADS
    kern = functools.partial(_conv_qkv_kernel, tm=tm, x_tiles=x_tiles, tiles_per_batch=tiles_per_batch,
                             ctx_len=ctx_len, qscale=float(dh) ** -0.5)
    row = pl.BlockSpec((tm, di), lambda i: (i, 0))
    return pl.pallas_call(
        kern,
        grid=(r // tm,),
        in_specs=[row,
                  pl.BlockSpec((w, di), lambda i: (jnp.maximum(i * hb - 1, 0), 0)),
                  pl.BlockSpec((w, di), lambda i: (jnp.minimum((i + 1) * hb, last_halo), 0)),
                  _const_spec(conv_w9.shape), _const_spec((1, di)),
                  _const_spec(wq.shape), _const_spec(wk.shape), _const_spec(wkt.shape),
                  _const_spec(wv.shape), _const_spec(wg.shape),
                  _const_spec((1, ng)), _const_spec((ng, 1))],
        out_specs=[row, row, pl.BlockSpec((di, tm), lambda i: (0, i)), row,
                   pl.BlockSpec((tm, ng), lambda i: (i, 0)), pl.BlockSpec((ng, tm), lambda i: (0, i))],
        out_shape=[jax.ShapeDtypeStruct((r, di), BF16),
                   jax.ShapeDtypeStruct((r, di), BF16),
                   jax.ShapeDtypeStruct((di, r), BF16),
                   jax.ShapeDtypeStruct((r, di), BF16),
                   jax.ShapeDtypeStruct((r, ng), F32),
                   jax.ShapeDtypeStruct((ng, r), F32)],
        scratch_shapes=[pltpu.VMEM((tm + 2 * w + 16, di), F32), pltpu.VMEM((tm, GATE_PAD), F32)],
        compiler_params=_params(1),
        name="mlstm_conv_qkv",
    )(xm, xm, xm, conv_w9, conv_b.reshape(1, di), wq, wk, wkt, wv, wg, bg.reshape(1, ng), bg.reshape(ng, 1))


def _scan_chain(q, kt, v, i_col, f_col, i_row, f_row, vis, vist, neg, c_ref, cb_ref, n_ref, m_ref):
    L = q.shape[0]
    lf_col = jax.nn.log_sigmoid(f_col)
    lf_row = jax.nn.log_sigmoid(f_row)
    b_col = jnp.sum(vis * lf_row, axis=1, keepdims=True)
    b_row = jnp.sum(vist * lf_col, axis=0, keepdims=True)
    b_tot = jnp.sum(lf_row, axis=1, keepdims=True)

    m_prev = m_ref[0:1, 0:1]
    log_d = (b_col - b_row + i_row) + neg
    m_inter = b_col + m_prev
    m_t = jnp.maximum(m_inter, jnp.max(log_d, axis=1, keepdims=True))
    dmat = jnp.exp(log_d - m_t)
    s = jnp.dot(q, kt, preferred_element_type=F32) * dmat
    scale_inter = jnp.exp(m_inter - m_t)
    inter = jnp.dot(q, cb_ref[...], preferred_element_type=F32)
    qn = jnp.dot(q, n_ref[...].astype(BF16), preferred_element_type=F32)[:, 0:1]
    num = jnp.dot(s.astype(BF16), v, preferred_element_type=F32) + scale_inter * inter
    den = jnp.sum(s, axis=1, keepdims=True) + scale_inter * qn
    h = (num / jnp.maximum(jnp.abs(den), jnp.exp(-m_t))).astype(BF16)

    log_w = b_tot - b_col + i_col
    m_new = jnp.maximum(b_tot + m_prev, jnp.max(log_w, axis=0, keepdims=True))
    wgt = jnp.exp(log_w - m_new)
    decay = jnp.exp(b_tot + m_prev - m_new)
    wv = (wgt * v.astype(F32)).astype(BF16)
    c_new = decay * c_ref[...] + jnp.dot(kt, wv, preferred_element_type=F32)
    c_ref[...] = c_new
    cb_ref[...] = c_new.astype(BF16)
    wb = jnp.broadcast_to(wgt, (L, n_ref.shape[1])).astype(BF16)
    n_ref[...] = decay * n_ref[...] + jnp.dot(kt, wb, preferred_element_type=F32)
    m_ref[...] = jnp.broadcast_to(m_new, m_ref.shape)
    return h


def _scan_kernel(qf_ref, ktf_ref, vf_ref, gf_ref, gtf_ref, qb_ref, ktb_ref, vb_ref, gb_ref, gtb_ref,
                 vis_ref, neg_ref, hf_ref, hb_ref, c_ref, cb_ref, n_ref, m_ref, *, hp):
    hg = pl.program_id(1)
    j = pl.program_id(2)

    @pl.when(j == 0)
    def _():
        c_ref[...] = jnp.zeros_like(c_ref)
        cb_ref[...] = jnp.zeros_like(cb_ref)
        n_ref[...] = jnp.zeros_like(n_ref)
        m_ref[...] = jnp.zeros_like(m_ref)

    L = qf_ref.shape[0]
    dh = qf_ref.shape[1] // hp
    streams = ((qf_ref, ktf_ref, vf_ref, gf_ref, gtf_ref, hf_ref), (qb_ref, ktb_ref, vb_ref, gb_ref, gtb_ref, hb_ref))
    for dr, (q_ref, kt_ref, v_ref, g_ref, gt_ref, h_ref) in enumerate(streams):
        g = g_ref[...]
        lane = lax.broadcasted_iota(jnp.int32, g.shape, 1)
        for hh in range(hp):
            head = hg * hp + hh
            i_idx = dr * 2 * A_HEADS + head
            f_idx = i_idx + A_HEADS
            i_col = jnp.sum(jnp.where(lane == i_idx, g, 0.0), axis=1, keepdims=True)
            f_col = jnp.sum(jnp.where(lane == f_idx, g, 0.0), axis=1, keepdims=True)
            i_row = gt_ref[pl.ds(i_idx, 1), :]
            f_row = gt_ref[pl.ds(f_idx, 1), :]
            cs = slice(hh * dh, (hh + 1) * dh)
            h_ref[:, cs] = _scan_chain(
                q_ref[:, cs], kt_ref[cs, :], v_ref[:, cs], i_col, f_col, i_row, f_row,
                vis_ref[dr], vis_ref[1 - dr], neg_ref[dr],
                c_ref.at[dr, hh], cb_ref.at[dr, hh], n_ref.at[dr, hh], m_ref.at[dr, hh])


def _mlstm_scan(q, kt, v, g, gt, *, rows_per_batch, n_batch, ctx_len):
    r, di = q.shape
    L = SCAN_CHUNK
    hp = SCAN_HEADS_PER_STEP
    dh = di // A_HEADS
    nctx = ctx_len // L
    nx = rows_per_batch // L
    ctx_base = n_batch * nx
    ngate = g.shape[1]

    def blk_f(b, j):
        return jnp.where(j < nctx, ctx_base + b * nctx + j, b * nx + (j - nctx))

    def blk_b(b, j):
        return jnp.where(j < nctx, ctx_base + b * nctx + (nctx - 1 - j), b * nx + (nx - 1 - (j - nctx)))

    def stream_specs(blk):
        tok = pl.BlockSpec((L, hp * dh), lambda b, h, j: (blk(b, j), h))
        return [tok, pl.BlockSpec((hp * dh, L), lambda b, h, j: (h, blk(b, j))), tok,
                pl.BlockSpec((L, ngate), lambda b, h, j: (blk(b, j), 0)),
                pl.BlockSpec((ngate, L), lambda b, h, j: (0, blk(b, j)))]

    tri = jnp.tril(jnp.ones((L, L), F32))
    vis = jnp.stack([tri, tri.T])
    neg = jnp.where(vis > 0, 0.0, -jnp.inf).astype(F32)
    out_f = pl.BlockSpec((L, hp * dh), lambda b, h, j: (blk_f(b, j), h))
    out_b = pl.BlockSpec((L, hp * dh), lambda b, h, j: (blk_b(b, j), h))
    return pl.pallas_call(
        functools.partial(_scan_kernel, hp=hp),
        grid=(n_batch, A_HEADS // hp, nctx + nx),
        in_specs=stream_specs(blk_f) + stream_specs(blk_b) + [_const_spec(vis.shape), _const_spec(neg.shape)],
        out_specs=[out_f, out_b],
        out_shape=[jax.ShapeDtypeStruct((r, di), BF16), jax.ShapeDtypeStruct((r, di), BF16)],
        scratch_shapes=[pltpu.VMEM((2, hp, dh, dh), F32), pltpu.VMEM((2, hp, dh, dh), BF16),
                        pltpu.VMEM((2, hp, dh, GATE_PAD), F32), pltpu.VMEM((2, hp, 8, 128), F32)],
        compiler_params=_params(3),
        name="mlstm_scan",
    )(q, kt, v, g, gt, q, kt, v, g, gt, vis, neg)


def _mlstm_out_kernel(hf_ref, hb_ref, xc_ref, z_ref, x_ref, m_ref, g_ref, ag_ref, sk_ref, w_ref, o_ref, acc_ref):
    di = xc_ref.shape[1]
    dh = di // A_HEADS
    for hd in range(A_HEADS):
        cs = slice(hd * dh, (hd + 1) * dh)
        h = hf_ref[:, cs].astype(F32) + hb_ref[:, cs].astype(F32)
        mu = jnp.mean(h, axis=-1, keepdims=True)
        hc = h - mu
        var = jnp.mean(hc * hc, axis=-1, keepdims=True)
        hn = hc * lax.rsqrt(var + EPS) * ag_ref[:, cs]
        t = ((hn + sk_ref[:, cs] * xc_ref[:, cs].astype(F32)) * _silu(z_ref[:, cs].astype(F32))).astype(BF16)
        part = jnp.dot(t, w_ref[cs, :], preferred_element_type=F32)
        if hd == 0:
            acc_ref[...] = part
        else:
            acc_ref[...] += part
    o_ref[...] = x_ref[...] + m_ref[5:6, :] * _rms(acc_ref[...], g_ref[3:4, :])


def _mlstm_out(hf, hb, xc, z, x, mod, ng, a_norm_g, a_skip, w_out, widx, *, rows_out, rows_per_batch, n_batch):
    di = xc.shape[1]
    d = x.shape[1]
    tm = MIX_TILE
    midx = _mod_index(tm, rows_per_batch, n_batch)
    row_di = pl.BlockSpec((tm, di), lambda i: (i, 0))
    row_d = pl.BlockSpec((tm, d), lambda i: (i, 0))
    return pl.pallas_call(
        _mlstm_out_kernel,
        grid=(rows_out // tm,),
        in_specs=[row_di, row_di, row_di, row_di, row_d,
                  pl.BlockSpec((None, N_MOD, d), lambda i: (midx(i), 0, 0)),
                  _const_spec(ng.shape), _const_spec((1, di)), _const_spec((1, di)), _layer_spec(w_out, widx)],
        out_specs=row_d,
        out_shape=jax.ShapeDtypeStruct((rows_out, d), F32),
        scratch_shapes=[pltpu.VMEM((tm, d), F32)],
        compiler_params=_params(1),
        name="mlstm_out",
    )(hf, hb, xc, z, x, mod, ng, a_norm_g.reshape(1, di), a_skip.reshape(1, di), w_out)


def _gelu(x):
    return 0.5 * x * (1.0 + lax.erf(x * (2.0 ** -0.5)))


def _gmlp_kernel(x_ref, m_ref, g_ref, win_ref, bin_ref, lng_ref, lnb_ref, ws_ref, bs_ref, wout_ref,
                 bout_ref, o_ref, *scr, e):
    tm = x_ref.shape[0]
    shift, scale = m_ref[3:4, :], m_ref[4:5, :]
    ck = 512
    ge = e // B_GROUPS
    sub = GMLP_SUB
    nsub = tm // sub

    def p1(sc):
        u_ref, v_ref, t_ref = scr[3 * sc:3 * sc + 3]
        x = x_ref[sc * sub:(sc + 1) * sub, :]
        xb = (_rms(x, g_ref[2:3, :]) * (1.0 + scale) + shift).astype(BF16)
        for c in range(e // ck):
            cs = slice(c * ck, (c + 1) * ck)
            cs2 = slice(e + c * ck, e + (c + 1) * ck)
            u_ref[:, cs] = _gelu(jnp.dot(xb, win_ref[:, cs], preferred_element_type=F32) + bin_ref[:, cs])
            v_ref[:, cs] = _gelu(jnp.dot(xb, win_ref[:, cs2], preferred_element_type=F32) + bin_ref[:, cs2])

    def p2(sc):
        u_ref, v_ref, t_ref = scr[3 * sc:3 * sc + 3]
        v = v_ref[...]
        mu = jnp.mean(v, axis=-1, keepdims=True)
        vc = v - mu
        var = jnp.mean(vc * vc, axis=-1, keepdims=True)
        v_ref[...] = vc * lax.rsqrt(var + EPS) * lng_ref[...] + lnb_ref[...]

    def p3(sc):
        u_ref, v_ref, t_ref = scr[3 * sc:3 * sc + 3]
        for rc in range(sub // B_CHUNK):
            rs = slice(rc * B_CHUNK, (rc + 1) * B_CHUNK)
            for g in range(B_GROUPS):
                cs = slice(g * ge, (g + 1) * ge)
                vb = jnp.dot(ws_ref[g], v_ref[rs, cs].astype(BF16), preferred_element_type=F32) + bs_ref[:, g:g + 1]
                t_ref[rs, cs] = (u_ref[rs, cs] * vb).astype(BF16)

    def p4(sc):
        u_ref, v_ref, t_ref = scr[3 * sc:3 * sc + 3]
        rows = slice(sc * sub, (sc + 1) * sub)
        y = jnp.dot(t_ref[...], wout_ref[...], preferred_element_type=F32) + bout_ref[...]
        o_ref[rows, :] = x_ref[rows, :] + m_ref[5:6, :] * _rms(y, g_ref[3:4, :])

    p1(0)
    for sc in range(nsub):
        if sc + 1 < nsub:
            p1(sc + 1)
        p2(sc)
        p3(sc)
        p4(sc)


def _gmlp(x, mod, ng, w_in, b_in, ln_g, ln_b, w_s, b_s_t, w_out, b_out, widx, *, rows_out, rows_per_batch,
          n_batch):
    d = x.shape[1]
    e = w_out.shape[-2]
    tm = GMLP_TILE
    midx = _mod_index(tm, rows_per_batch, n_batch)
    row_d = pl.BlockSpec((tm, d), lambda i: (i, 0))
    return pl.pallas_call(
        functools.partial(_gmlp_kernel, e=e),
        grid=(rows_out // tm,),
        in_specs=[row_d, pl.BlockSpec((None, N_MOD, d), lambda i: (midx(i), 0, 0)),
                  _const_spec(ng.shape), _layer_spec(w_in, widx), _const_spec((1, 2 * e)),
                  _const_spec((1, e)), _const_spec((1, e)), _const_spec(w_s.shape), _const_spec(b_s_t.shape),
                  _layer_spec(w_out, widx), _const_spec((1, d))],
        out_specs=row_d,
        out_shape=jax.ShapeDtypeStruct((rows_out, d), F32),
        scratch_shapes=[pltpu.VMEM((GMLP_SUB, e), F32), pltpu.VMEM((GMLP_SUB, e), F32),
                        pltpu.VMEM((GMLP_SUB, e), BF16)] * (tm // GMLP_SUB),
        compiler_params=_params(1),
        name="gmlp_mixer",
    )(x, mod, ng, w_in, b_in.reshape(1, 2 * e), ln_g.reshape(1, e), ln_b.reshape(1, e), w_s, b_s_t,
      w_out, b_out.reshape(1, d))


def _dense_blockdiag(w, transposed=False):
    nb, bo, bi = w.shape
    per = QKV_GROUP // bi
    wg = w.reshape(nb // per, per, bo, bi)
    if transposed:
        base = wg.reshape(nb // per, per * bo, bi)
        blk = bi
    else:
        base = jnp.swapaxes(wg, 2, 3).reshape(nb // per, per * bi, bo)
        blk = bo
    tiled = jnp.tile(base, (1, 1, per))
    rows = lax.broadcasted_iota(jnp.int32, tiled.shape[1:], 0) // (tiled.shape[1] // per)
    cols = lax.broadcasted_iota(jnp.int32, tiled.shape[1:], 1) // blk
    return jnp.where(rows == cols, tiled, 0.0).astype(BF16)


def kernel(x, c, ctx, c_ctx, w_ada, b_ada, norm_g, ffn_w_in, ffn_w_out, a_w_in, a_conv_w, a_conv_b, a_w_qkv, a_w_gates, a_b_gates, a_norm_g, a_skip, a_w_out, b_w_in, b_b_in, b_ln_g, b_ln_b, b_w_s, b_b_s, b_w_out, b_b_out):
    n_batch, seq, d = x.shape
    ctx_len = ctx.shape[1]
    depth = w_ada.shape[0]
    rows_x = n_batch * seq
    rows_all = rows_x + n_batch * ctx_len
    di = a_w_out.shape[1]
    assert seq % ROW_TILE == 0 and (n_batch * ctx_len) % ROW_TILE == 0 and ctx_len % MIX_TILE == 0
    assert seq % GRID_W == 0 and MIX_TILE % GRID_W == 0 and ctx_len % SCAN_CHUNK == 0
    assert ctx_len & (ctx_len - 1) == 0 and a_w_gates.shape[2] == 4 * A_HEADS

    cond = jnp.concatenate([c, c_ctx[None], jnp.zeros((8 - n_batch - 1, d), F32)], axis=0)
    mods = _ada_table(cond, w_ada, b_ada).reshape(depth, 8, N_MOD, d)

    h = jnp.concatenate([x.reshape(rows_x, d), ctx.reshape(n_batch * ctx_len, d)], axis=0)
    common = dict(rows_per_batch=seq, n_batch=n_batch)
    ffn_w_in, ffn_w_out = ffn_w_in.astype(BF16), ffn_w_out.astype(BF16)
    a_w_in, a_w_out = a_w_in.astype(BF16), a_w_out.astype(BF16)
    b_w_in, b_w_out = b_w_in.astype(BF16), b_w_out.astype(BF16)
    mixer_layers = [l for l in range(depth) if l % 2 == 0]
    for l in range(depth):
        li = l // 2
        need_in = any(j >= l for j in mixer_layers)
        need_out = any(j > l for j in mixer_layers)
        rows_in = rows_all if need_in else rows_x
        rows_out = rows_all if need_out else rows_x
        mod, ng = mods[l], norm_g[l]
        h = _ffn(h, mod, ng, ffn_w_in, ffn_w_out, (l, 0),
                 mi=0, gi=0, rows_out=rows_in, **common)
        if l % 2 == 0:
            xm, z = _mlstm_inproj(h, mod, ng, a_w_in, (li,), **common)
            wqkv = a_w_qkv[li]
            wq, wk, wv = (_dense_blockdiag(wqkv[t]) for t in range(3))
            wkt = _dense_blockdiag(wqkv[1], transposed=True)
            wg = jnp.pad(a_w_gates[li].reshape(3, di, -1),
                         ((0, 0), (0, 0), (0, GATE_PAD - a_w_gates.shape[2]))).astype(BF16)
            xc, q, kt, v, g, gt = _mlstm_conv_qkv(
                xm, a_conv_w[li].reshape(9, di), a_conv_b[li], wq, wk, wkt, wv, wg, a_b_gates[li],
                ctx_len=ctx_len, **common)
            hf, hb = _mlstm_scan(q, kt, v, g, gt, ctx_len=ctx_len, **common)
            h = _mlstm_out(hf, hb, xc, z, h, mod, ng, a_norm_g[li], a_skip[li], a_w_out, (li,),
                           rows_out=rows_out, **common)
        else:
            h = _gmlp(h, mod, ng, b_w_in, b_b_in[li], b_ln_g[li], b_ln_b[li],
                      b_w_s[li].astype(BF16), b_b_s[li].T, b_w_out, b_b_out[li], (li,),
                      rows_out=rows_out, **common)
        h = _ffn(h, mod, ng, ffn_w_in, ffn_w_out, (l, 1),
                 mi=6, gi=4, rows_out=rows_out, **common)
    return h.reshape(n_batch, seq, d)
```

```python
import functools

import jax
import jax.numpy as jnp
from jax import lax
from jax.experimental import pallas as pl
from jax.experimental.pallas import tpu as pltpu

F32 = jnp.float32
BF16 = jnp.bfloat16

EPS = 1e-6
N_MOD = 9
GRID_W = 64
A_HEADS = 4
A_QKV_BLOCK = 4
B_CHUNK = 128
B_GROUPS = 8

ROW_TILE = 512
MIX_TILE = 512
GMLP_TILE = 512
GMLP_SUB = 256
SCAN_CHUNK = 256
SCAN_HEADS_PER_STEP = 4
FF_CHUNK = 256
QKV_GROUP = 256
GATE_PAD = 128
VMEM_LIMIT = 56 * 1024 * 1024


def _params(n_axes):
    return pltpu.CompilerParams(dimension_semantics=("arbitrary",) * n_axes,
                                vmem_limit_bytes=VMEM_LIMIT)


def _const_spec(shape):
    nd = len(shape)
    return pl.BlockSpec(shape, lambda *_: (0,) * nd, pipeline_mode=pl.Buffered(1))


def _layer_spec(stack, idx):
    lead = len(idx)
    return pl.BlockSpec((None,) * lead + tuple(stack.shape[lead:]), lambda *_: tuple(idx) + (0, 0),
                        pipeline_mode=pl.Buffered(1))


def _rms(x, g):
    return x * lax.rsqrt(jnp.mean(x * x, axis=-1, keepdims=True) + EPS) * g


def _silu(x):
    return x * jax.nn.sigmoid(x)


def _mod_index(tile_rows, rows_per_batch, n_batch):
    tiles_per_batch = rows_per_batch // tile_rows

    def index(i):
        return jnp.minimum(i // tiles_per_batch, n_batch)

    return index


def _ada_kernel(c_ref, w_ref, b_ref, o_ref):
    c = c_ref[...]
    a = _silu(c).astype(BF16)
    o_ref[...] = jnp.dot(a, w_ref[...].astype(BF16), preferred_element_type=F32) + b_ref[...]


def _ada_table(cond, w_ada, b_ada):
    depth, d, nd = w_ada.shape
    tn = 1024
    return pl.pallas_call(
        _ada_kernel,
        grid=(depth, nd // tn),
        in_specs=[pl.BlockSpec((8, d), lambda l, j: (0, 0)),
                  pl.BlockSpec((None, d, tn), lambda l, j: (l, 0, j)),
                  pl.BlockSpec((None, 1, tn), lambda l, j: (l, 0, j))],
        out_specs=pl.BlockSpec((None, 8, tn), lambda l, j: (l, 0, j)),
        out_shape=jax.ShapeDtypeStruct((depth, 8, nd), F32),
        compiler_params=_params(2),
        name="ada_table",
    )(cond, w_ada, b_ada.reshape(depth, 1, nd))


def _ffn_kernel(x_ref, *rest, mi, gi, dff, head_tiles):
    if head_tiles is None:
        m_ref, g_ref, win_ref, wout_ref, o_ref, acc_ref = rest
        x = x_ref[...]
    else:
        tail_ref, m_ref, g_ref, win_ref, wout_ref, o_ref, acc_ref = rest
        x = jnp.where(pl.program_id(0) < head_tiles, x_ref[...], tail_ref[...])
    shift, scale, gate = m_ref[mi:mi + 1, :], m_ref[mi + 1:mi + 2, :], m_ref[mi + 2:mi + 3, :]
    xb = (_rms(x, g_ref[gi:gi + 1, :]) * (1.0 + scale) + shift).astype(BF16)
    for c in range(dff // FF_CHUNK):
        lo = c * FF_CHUNK
        hg = jnp.dot(xb, win_ref[:, lo:lo + FF_CHUNK], preferred_element_type=F32)
        hu = jnp.dot(xb, win_ref[:, dff + lo:dff + lo + FF_CHUNK], preferred_element_type=F32)
        a = (_silu(hg) * hu).astype(BF16)
        part = jnp.dot(a, wout_ref[lo:lo + FF_CHUNK, :], preferred_element_type=F32)
        if c == 0:
            acc_ref[...] = part
        else:
            acc_ref[...] += part
    o_ref[...] = x + 0.5 * gate * _rms(acc_ref[...], g_ref[gi + 1:gi + 2, :])


def _ffn(x, mod, ng, w_in, w_out, widx, *, mi, gi, rows_out, rows_per_batch, n_batch, tail=None):
    d = x.shape[1]
    dff = w_out.shape[-2]
    tm = ROW_TILE
    midx = _mod_index(tm, rows_per_batch, n_batch)
    if tail is None:
        head_tiles = None
        rows = [x]
        row_specs = [pl.BlockSpec((tm, d), lambda i: (i, 0))]
    else:
        head_tiles = x.shape[0] // tm
        rows = [x, tail]
        row_specs = [pl.BlockSpec((tm, d), lambda i: (jnp.minimum(i, head_tiles - 1), 0)),
                     pl.BlockSpec((tm, d), lambda i: (jnp.maximum(i - head_tiles, 0), 0))]
    return pl.pallas_call(
        functools.partial(_ffn_kernel, mi=mi, gi=gi, dff=dff, head_tiles=head_tiles),
        grid=(rows_out // tm,),
        in_specs=row_specs + [
                  pl.BlockSpec((None, N_MOD, d), lambda i: (midx(i), 0, 0)),
                  _const_spec(ng.shape), _layer_spec(w_in, widx), _layer_spec(w_out, widx)],
        out_specs=pl.BlockSpec((tm, d), lambda i: (i, 0)),
        out_shape=jax.ShapeDtypeStruct((rows_out, d), F32),
        scratch_shapes=[pltpu.VMEM((tm, d), F32)],
        compiler_params=_params(1),
        name="ffn_half_step",
    )(*rows, mod, ng, w_in, w_out)


def _inproj_kernel(x_ref, m_ref, g_ref, w_ref, xm_ref, z_ref, *, di):
    x = x_ref[...]
    shift, scale = m_ref[3:4, :], m_ref[4:5, :]
    xb = (_rms(x, g_ref[2:3, :]) * (1.0 + scale) + shift).astype(BF16)
    ck = 1024
    for c in range(di // ck):
        xm_ref[:, c * ck:(c + 1) * ck] = jnp.dot(xb, w_ref[:, c * ck:(c + 1) * ck],
                                                 preferred_element_type=F32).astype(BF16)
        z_ref[:, c * ck:(c + 1) * ck] = jnp.dot(xb, w_ref[:, di + c * ck:di + (c + 1) * ck],
                                                preferred_element_type=F32).astype(BF16)


def _mlstm_inproj(x, mod, ng, w_in, widx, *, rows_per_batch, n_batch):
    r, d = x.shape
    di = w_in.shape[-1] // 2
    tm = ROW_TILE
    midx = _mod_index(tm, rows_per_batch, n_batch)
    return pl.pallas_call(
        functools.partial(_inproj_kernel, di=di),
        grid=(r // tm,),
        in_specs=[pl.BlockSpec((tm, d), lambda i: (i, 0)),
                  pl.BlockSpec((None, N_MOD, d), lambda i: (midx(i), 0, 0)),
                  _const_spec(ng.shape), _layer_spec(w_in, widx)],
        out_specs=[pl.BlockSpec((tm, di), lambda i: (i, 0)), pl.BlockSpec((tm, di), lambda i: (i, 0))],
        out_shape=[jax.ShapeDtypeStruct((r, di), BF16), jax.ShapeDtypeStruct((r, di), BF16)],
        compiler_params=_params(1),
        name="mlstm_inproj",
    )(x, mod, ng, w_in)


def _conv_qkv_kernel(xm_ref, top_ref, bot_ref, cw_ref, cb_ref, wq_ref, wk_ref, wkt_ref, wv_ref,
                     wg_ref, bg_ref, bgc_ref,
                     xc_ref, q_ref, kt_ref, v_ref, g_ref, gt_ref,
                     ext_ref, gacc_ref, *, tm, x_tiles, tiles_per_batch, ctx_len, qscale):
    i = pl.program_id(0)
    di = xm_ref.shape[1]
    is_ctx = i >= x_tiles
    jloc = i % tiles_per_batch
    top_ok = jnp.logical_and(jnp.logical_not(is_ctx), jloc > 0)
    bot_ok = jnp.logical_and(jnp.logical_not(is_ctx), jloc < tiles_per_batch - 1)
    w = GRID_W
    pad = 8
    base = pad + w
    ext_ref[0:pad, :] = jnp.zeros((pad, di), F32)
    ext_ref[pad:base, :] = top_ref[...].astype(F32) * jnp.where(top_ok, 1.0, 0.0).astype(F32)
    ext_ref[base:base + tm, :] = xm_ref[...].astype(F32)
    ext_ref[base + tm:base + tm + w, :] = bot_ref[...].astype(F32) * jnp.where(bot_ok, 1.0, 0.0).astype(F32)
    ext_ref[base + tm + w:base + tm + w + pad, :] = jnp.zeros((pad, di), F32)

    gq = QKV_GROUP
    r_idx = lax.broadcasted_iota(jnp.int32, (tm, gq), 0)
    col = jnp.where(is_ctx, r_idx & (ctx_len - 1), r_idx & (w - 1))
    last = jnp.where(is_ctx, ctx_len - 1, w - 1)
    left_ok = (col != 0).astype(F32)
    right_ok = (col != last).astype(F32)
    vert = jnp.where(is_ctx, 0.0, 1.0).astype(F32)

    for g in range(di // gq):
        cs = slice(g * gq, (g + 1) * gq)

        def tap(dy, dx):
            o = base + w * dy + dx
            return ext_ref[o:o + tm, cs]

        def wt(ky, kx):
            row = cw_ref[3 * ky + kx:3 * ky + kx + 1, cs]
            return row if ky == 1 else vert * row

        mid = wt(1, 1) * tap(0, 0) + wt(0, 1) * tap(-1, 0) + wt(2, 1) * tap(1, 0)
        lft = wt(1, 0) * tap(0, -1) + wt(0, 0) * tap(-1, -1) + wt(2, 0) * tap(1, -1)
        rgt = wt(1, 2) * tap(0, 1) + wt(0, 2) * tap(-1, 1) + wt(2, 2) * tap(1, 1)
        conv = mid + left_ok * lft + right_ok * rgt + cb_ref[:, cs]
        xc = _silu(conv)
        xcb = xc.astype(BF16)
        xc_ref[:, cs] = xcb
        xmb = xm_ref[:, cs]
        q = jnp.dot(xcb, wq_ref[g], preferred_element_type=F32)
        k = jnp.dot(xcb, wk_ref[g], preferred_element_type=F32)
        v = jnp.dot(xmb, wv_ref[g], preferred_element_type=F32)
        kt = lax.dot_general(wkt_ref[g], xcb, (((1,), (1,)), ((), ())), preferred_element_type=F32)
        qb, kb, vb = q.astype(BF16), k.astype(BF16), v.astype(BF16)
        q_ref[:, cs] = (q * qscale).astype(BF16)
        v_ref[:, cs] = vb
        kt_ref[cs, :] = kt.astype(BF16)
        part = (jnp.dot(qb, wg_ref[0, cs, :], preferred_element_type=F32)
                + jnp.dot(kb, wg_ref[1, cs, :], preferred_element_type=F32)
                + jnp.dot(vb, wg_ref[2, cs, :], preferred_element_type=F32))
        if g == 0:
            gacc_ref[...] = part
        else:
            gacc_ref[...] += part
    ng = g_ref.shape[1]
    acc = gacc_ref[...]
    g_ref[...] = acc[:, 0:ng] + bg_ref[...]
    gt_ref[...] = acc.T[0:ng, :] + bgc_ref[...]


def _mlstm_conv_qkv(xm, conv_w9, conv_b, wq, wk, wkt, wv, wg, bg, *, rows_per_batch, n_batch, ctx_len):
    r, di = xm.shape
    tm = MIX_TILE
    w = GRID_W
    x_tiles = n_batch * rows_per_batch // tm
    tiles_per_batch = rows_per_batch // tm
    hb = tm // w
    last_halo = r // w - 1
    ng = bg.shape[0]
    dh = di // A_HEADS
    kern = functools.partial(_conv_qkv_kernel, tm=tm, x_tiles=x_tiles, tiles_per_batch=tiles_per_batch,
                             ctx_len=ctx_len, qscale=float(dh) ** -0.5)
    row = pl.BlockSpec((tm, di), lambda i: (i, 0))
    return pl.pallas_call(
        kern,
        grid=(r // tm,),
        in_specs=[row,
                  pl.BlockSpec((w, di), lambda i: (jnp.maximum(i * hb - 1, 0), 0)),
                  pl.BlockSpec((w, di), lambda i: (jnp.minimum((i + 1) * hb, last_halo), 0)),
                  _const_spec(conv_w9.shape), _const_spec((1, di)),
                  _const_spec(wq.shape), _const_spec(wk.shape), _const_spec(wkt.shape),
                  _const_spec(wv.shape), _const_spec(wg.shape),
                  _const_spec((1, ng)), _const_spec((ng, 1))],
        out_specs=[row, row, pl.BlockSpec((di, tm), lambda i: (0, i)), row,
                   pl.BlockSpec((tm, ng), lambda i: (i, 0)), pl.BlockSpec((ng, tm), lambda i: (0, i))],
        out_shape=[jax.ShapeDtypeStruct((r, di), BF16),
                   jax.ShapeDtypeStruct((r, di), BF16),
                   jax.ShapeDtypeStruct((di, r), BF16),
                   jax.ShapeDtypeStruct((r, di), BF16),
                   jax.ShapeDtypeStruct((r, ng), F32),
                   jax.ShapeDtypeStruct((ng, r), F32)],
        scratch_shapes=[pltpu.VMEM((tm + 2 * w + 16, di), F32), pltpu.VMEM((tm, GATE_PAD), F32)],
        compiler_params=_params(1),
        name="mlstm_conv_qkv",
    )(xm, xm, xm, conv_w9, conv_b.reshape(1, di), wq, wk, wkt, wv, wg, bg.reshape(1, ng), bg.reshape(ng, 1))


def _scan_chain(q, kt, v, i_col, f_col, i_row, f_row, vis, vist, neg, c_ref, cb_ref, n_ref, m_ref):
    L = q.shape[0]
    lf_col = jax.nn.log_sigmoid(f_col)
    lf_row = jax.nn.log_sigmoid(f_row)
    b_col = jnp.sum(vis * lf_row, axis=1, keepdims=True)
    b_row = jnp.sum(vist * lf_col, axis=0, keepdims=True)
    b_tot = jnp.sum(lf_row, axis=1, keepdims=True)

    m_prev = m_ref[0:1, 0:1]
    log_d = (b_col - b_row + i_row) + neg
    m_inter = b_col + m_prev
    m_t = jnp.maximum(m_inter, jnp.max(log_d, axis=1, keepdims=True))
    dmat = jnp.exp(log_d - m_t)
    s = jnp.dot(q, kt, preferred_element_type=F32) * dmat
    scale_inter = jnp.exp(m_inter - m_t)
    inter = jnp.dot(q, cb_ref[...], preferred_element_type=F32)
    qn = jnp.dot(q, n_ref[...].astype(BF16), preferred_element_type=F32)[:, 0:1]
    num = jnp.dot(s.astype(BF16), v, preferred_element_type=F32) + scale_inter * inter
    den = jnp.sum(s, axis=1, keepdims=True) + scale_inter * qn
    h = (num / jnp.maximum(jnp.abs(den), jnp.exp(-m_t))).astype(BF16)

    log_w = b_tot - b_col + i_col
    m_new = jnp.maximum(b_tot + m_prev, jnp.max(log_w, axis=0, keepdims=True))
    wgt = jnp.exp(log_w - m_new)
    decay = jnp.exp(b_tot + m_prev - m_new)
    wv = (wgt * v.astype(F32)).astype(BF16)
    c_new = decay * c_ref[...] + jnp.dot(kt, wv, preferred_element_type=F32)
    c_ref[...] = c_new
    cb_ref[...] = c_new.astype(BF16)
    wb = jnp.broadcast_to(wgt, (L, n_ref.shape[1])).astype(BF16)
    n_ref[...] = decay * n_ref[...] + jnp.dot(kt, wb, preferred_element_type=F32)
    m_ref[...] = jnp.broadcast_to(m_new, m_ref.shape)
    return h


def _scan_kernel(qf_ref, ktf_ref, vf_ref, gf_ref, gtf_ref, qb_ref, ktb_ref, vb_ref, gb_ref, gtb_ref,
                 vis_ref, neg_ref, hf_ref, hb_ref, c_ref, cb_ref, n_ref, m_ref, *, hp):
    hg = pl.program_id(1)
    j = pl.program_id(2)

    @pl.when(j == 0)
    def _():
        c_ref[...] = jnp.zeros_like(c_ref)
        cb_ref[...] = jnp.zeros_like(cb_ref)
        n_ref[...] = jnp.zeros_like(n_ref)
        m_ref[...] = jnp.zeros_like(m_ref)

    L = qf_ref.shape[0]
    dh = qf_ref.shape[1] // hp
    streams = ((qf_ref, ktf_ref, vf_ref, gf_ref, gtf_ref, hf_ref), (qb_ref, ktb_ref, vb_ref, gb_ref, gtb_ref, hb_ref))
    for dr, (q_ref, kt_ref, v_ref, g_ref, gt_ref, h_ref) in enumerate(streams):
        g = g_ref[...]
        lane = lax.broadcasted_iota(jnp.int32, g.shape, 1)
        for hh in range(hp):
            if hp == A_HEADS:
                i_idx = dr * 2 * A_HEADS + hh
                f_idx = i_idx + A_HEADS
                i_col, f_col = g[:, i_idx:i_idx + 1], g[:, f_idx:f_idx + 1]
                i_row, f_row = gt_ref[i_idx:i_idx + 1, :], gt_ref[f_idx:f_idx + 1, :]
            else:
                i_idx = dr * 2 * A_HEADS + hg * hp + hh
                f_idx = i_idx + A_HEADS
                i_col = jnp.sum(jnp.where(lane == i_idx, g, 0.0), axis=1, keepdims=True)
                f_col = jnp.sum(jnp.where(lane == f_idx, g, 0.0), axis=1, keepdims=True)
                i_row = gt_ref[pl.ds(i_idx, 1), :]
                f_row = gt_ref[pl.ds(f_idx, 1), :]
            cs = slice(hh * dh, (hh + 1) * dh)
            h_ref[:, cs] = _scan_chain(
                q_ref[:, cs], kt_ref[cs, :], v_ref[:, cs], i_col, f_col, i_row, f_row,
                vis_ref[dr], vis_ref[1 - dr], neg_ref[dr],
                c_ref.at[dr, hh], cb_ref.at[dr, hh], n_ref.at[dr, hh], m_ref.at[dr, hh])


def _mlstm_scan(q, kt, v, g, gt, *, rows_per_batch, n_batch, ctx_len):
    r, di = q.shape
    L = SCAN_CHUNK
    hp = SCAN_HEADS_PER_STEP
    dh = di // A_HEADS
    nctx = ctx_len // L
    nx = rows_per_batch // L
    ctx_base = n_batch * nx
    ngate = g.shape[1]

    def blk_f(b, j):
        return jnp.where(j < nctx, ctx_base + b * nctx + j, b * nx + (j - nctx))

    def blk_b(b, j):
        return jnp.where(j < nctx, ctx_base + b * nctx + (nctx - 1 - j), b * nx + (nx - 1 - (j - nctx)))

    def stream_specs(blk):
        tok = pl.BlockSpec((L, hp * dh), lambda b, h, j: (blk(b, j), h))
        return [tok, pl.BlockSpec((hp * dh, L), lambda b, h, j: (h, blk(b, j))), tok,
                pl.BlockSpec((L, ngate), lambda b, h, j: (blk(b, j), 0)),
                pl.BlockSpec((ngate, L), lambda b, h, j: (0, blk(b, j)))]

    tri = jnp.tril(jnp.ones((L, L), F32))
    vis = jnp.stack([tri, tri.T])
    neg = jnp.where(vis > 0, 0.0, -jnp.inf).astype(F32)
    out_f = pl.BlockSpec((L, hp * dh), lambda b, h, j: (blk_f(b, j), h))
    out_b = pl.BlockSpec((L, hp * dh), lambda b, h, j: (blk_b(b, j), h))
    return pl.pallas_call(
        functools.partial(_scan_kernel, hp=hp),
        grid=(n_batch, A_HEADS // hp, nctx + nx),
        in_specs=stream_specs(blk_f) + stream_specs(blk_b) + [_const_spec(vis.shape), _const_spec(neg.shape)],
        out_specs=[out_f, out_b],
        out_shape=[jax.ShapeDtypeStruct((r, di), BF16), jax.ShapeDtypeStruct((r, di), BF16)],
        scratch_shapes=[pltpu.VMEM((2, hp, dh, dh), F32), pltpu.VMEM((2, hp, dh, dh), BF16),
                        pltpu.VMEM((2, hp, dh, GATE_PAD), F32), pltpu.VMEM((2, hp, 8, 128), F32)],
        compiler_params=_params(3),
        name="mlstm_scan",
    )(q, kt, v, g, gt, q, kt, v, g, gt, vis, neg)


def _mlstm_out_kernel(hf_ref, hb_ref, xc_ref, z_ref, x_ref, m_ref, g_ref, ag_ref, sk_ref, w_ref, o_ref, acc_ref):
    di = xc_ref.shape[1]
    dh = di // A_HEADS
    for hd in range(A_HEADS):
        cs = slice(hd * dh, (hd + 1) * dh)
        h = hf_ref[:, cs].astype(F32) + hb_ref[:, cs].astype(F32)
        mu = jnp.mean(h, axis=-1, keepdims=True)
        hc = h - mu
        var = jnp.mean(hc * hc, axis=-1, keepdims=True)
        hn = hc * lax.rsqrt(var + EPS) * ag_ref[:, cs]
        t = ((hn + sk_ref[:, cs] * xc_ref[:, cs].astype(F32)) * _silu(z_ref[:, cs].astype(F32))).astype(BF16)
        part = jnp.dot(t, w_ref[cs, :], preferred_element_type=F32)
        if hd == 0:
            acc_ref[...] = part
        else:
            acc_ref[...] += part
    o_ref[...] = x_ref[...] + m_ref[5:6, :] * _rms(acc_ref[...], g_ref[3:4, :])


def _mlstm_out(hf, hb, xc, z, x, mod, ng, a_norm_g, a_skip, w_out, widx, *, rows_out, rows_per_batch, n_batch):
    di = xc.shape[1]
    d = x.shape[1]
    tm = MIX_TILE
    midx = _mod_index(tm, rows_per_batch, n_batch)
    row_di = pl.BlockSpec((tm, di), lambda i: (i, 0))
    row_d = pl.BlockSpec((tm, d), lambda i: (i, 0))
    return pl.pallas_call(
        _mlstm_out_kernel,
        grid=(rows_out // tm,),
        in_specs=[row_di, row_di, row_di, row_di, row_d,
                  pl.BlockSpec((None, N_MOD, d), lambda i: (midx(i), 0, 0)),
                  _const_spec(ng.shape), _const_spec((1, di)), _const_spec((1, di)), _layer_spec(w_out, widx)],
        out_specs=row_d,
        out_shape=jax.ShapeDtypeStruct((rows_out, d), F32),
        scratch_shapes=[pltpu.VMEM((tm, d), F32)],
        compiler_params=_params(1),
        name="mlstm_out",
    )(hf, hb, xc, z, x, mod, ng, a_norm_g.reshape(1, di), a_skip.reshape(1, di), w_out)


def _gelu(x):
    return 0.5 * x * (1.0 + lax.erf(x * (2.0 ** -0.5)))


def _gmlp_kernel(x_ref, m_ref, g_ref, win_ref, bin_ref, lng_ref, lnb_ref, ws_ref, bs_ref, wout_ref,
                 bout_ref, o_ref, *scr, e):
    tm = x_ref.shape[0]
    shift, scale = m_ref[3:4, :], m_ref[4:5, :]
    ck = 512
    ge = e // B_GROUPS
    sub = GMLP_SUB
    nsub = tm // sub

    def p1(sc):
        u_ref, v_ref, t_ref = scr[3 * sc:3 * sc + 3]
        x = x_ref[sc * sub:(sc + 1) * sub, :]
        xb = (_rms(x, g_ref[2:3, :]) * (1.0 + scale) + shift).astype(BF16)
        for c in range(e // ck):
            cs = slice(c * ck, (c + 1) * ck)
            cs2 = slice(e + c * ck, e + (c + 1) * ck)
            u_ref[:, cs] = _gelu(jnp.dot(xb, win_ref[:, cs], preferred_element_type=F32) + bin_ref[:, cs])
            v_ref[:, cs] = _gelu(jnp.dot(xb, win_ref[:, cs2], preferred_element_type=F32) + bin_ref[:, cs2])

    def p2(sc):
        u_ref, v_ref, t_ref = scr[3 * sc:3 * sc + 3]
        v = v_ref[...]
        mu = jnp.mean(v, axis=-1, keepdims=True)
        vc = v - mu
        var = jnp.mean(vc * vc, axis=-1, keepdims=True)
        v_ref[...] = vc * lax.rsqrt(var + EPS) * lng_ref[...] + lnb_ref[...]

    def p3(sc):
        u_ref, v_ref, t_ref = scr[3 * sc:3 * sc + 3]
        for rc in range(sub // B_CHUNK):
            rs = slice(rc * B_CHUNK, (rc + 1) * B_CHUNK)
            for g in range(B_GROUPS):
                cs = slice(g * ge, (g + 1) * ge)
                vb = jnp.dot(ws_ref[g], v_ref[rs, cs].astype(BF16), preferred_element_type=F32) + bs_ref[:, g:g + 1]
                t_ref[rs, cs] = (u_ref[rs, cs] * vb).astype(BF16)

    def p4(sc):
        u_ref, v_ref, t_ref = scr[3 * sc:3 * sc + 3]
        rows = slice(sc * sub, (sc + 1) * sub)
        y = jnp.dot(t_ref[...], wout_ref[...], preferred_element_type=F32) + bout_ref[...]
        o_ref[rows, :] = x_ref[rows, :] + m_ref[5:6, :] * _rms(y, g_ref[3:4, :])

    p1(0)
    for sc in range(nsub):
        if sc + 1 < nsub:
            p1(sc + 1)
        p2(sc)
        p3(sc)
        p4(sc)


def _gmlp(x, mod, ng, w_in, b_in, ln_g, ln_b, w_s, b_s_t, w_out, b_out, widx, *, rows_out, rows_per_batch,
          n_batch):
    d = x.shape[1]
    e = w_out.shape[-2]
    tm = GMLP_TILE
    midx = _mod_index(tm, rows_per_batch, n_batch)
    row_d = pl.BlockSpec((tm, d), lambda i: (i, 0))
    return pl.pallas_call(
        functools.partial(_gmlp_kernel, e=e),
        grid=(rows_out // tm,),
        in_specs=[row_d, pl.BlockSpec((None, N_MOD, d), lambda i: (midx(i), 0, 0)),
                  _const_spec(ng.shape), _layer_spec(w_in, widx), _const_spec((1, 2 * e)),
                  _const_spec((1, e)), _const_spec((1, e)), _const_spec(w_s.shape), _const_spec(b_s_t.shape),
                  _layer_spec(w_out, widx), _const_spec((1, d))],
        out_specs=row_d,
        out_shape=jax.ShapeDtypeStruct((rows_out, d), F32),
        scratch_shapes=[pltpu.VMEM((GMLP_SUB, e), F32), pltpu.VMEM((GMLP_SUB, e), F32),
                        pltpu.VMEM((GMLP_SUB, e), BF16)] * (tm // GMLP_SUB),
        compiler_params=_params(1),
        name="gmlp_mixer",
    )(x, mod, ng, w_in, b_in.reshape(1, 2 * e), ln_g.reshape(1, e), ln_b.reshape(1, e), w_s, b_s_t,
      w_out, b_out.reshape(1, d))


def _dense_blockdiag(w, transposed=False):
    nb, bo, bi = w.shape
    per = QKV_GROUP // bi
    wg = w.reshape(nb // per, per, bo, bi)
    if transposed:
        base = wg.reshape(nb // per, per * bo, bi)
        blk = bi
    else:
        base = jnp.swapaxes(wg, 2, 3).reshape(nb // per, per * bi, bo)
        blk = bo
    tiled = jnp.tile(base, (1, 1, per))
    rows = lax.broadcasted_iota(jnp.int32, tiled.shape[1:], 0) // (tiled.shape[1] // per)
    cols = lax.broadcasted_iota(jnp.int32, tiled.shape[1:], 1) // blk
    return jnp.where(rows == cols, tiled, 0.0).astype(BF16)


def kernel(x, c, ctx, c_ctx, w_ada, b_ada, norm_g, ffn_w_in, ffn_w_out, a_w_in, a_conv_w, a_conv_b, a_w_qkv, a_w_gates, a_b_gates, a_norm_g, a_skip, a_w_out, b_w_in, b_b_in, b_ln_g, b_ln_b, b_w_s, b_b_s, b_w_out, b_b_out):
    n_batch, seq, d = x.shape
    ctx_len = ctx.shape[1]
    depth = w_ada.shape[0]
    rows_x = n_batch * seq
    rows_all = rows_x + n_batch * ctx_len
    di = a_w_out.shape[1]
    assert seq % ROW_TILE == 0 and (n_batch * ctx_len) % ROW_TILE == 0 and (n_batch * ctx_len) % MIX_TILE == 0
    assert MIX_TILE % ctx_len == 0 or ctx_len % MIX_TILE == 0
    assert seq % GRID_W == 0 and MIX_TILE % GRID_W == 0 and ctx_len % SCAN_CHUNK == 0
    assert ctx_len & (ctx_len - 1) == 0 and a_w_gates.shape[2] == 4 * A_HEADS

    cond = jnp.concatenate([c, c_ctx[None], jnp.zeros((8 - n_batch - 1, d), F32)], axis=0)
    mods = _ada_table(cond, w_ada, b_ada).reshape(depth, 8, N_MOD, d)

    h = x.reshape(rows_x, d)
    h_tail = ctx.reshape(n_batch * ctx_len, d)
    common = dict(rows_per_batch=seq, n_batch=n_batch)
    ffn_w_in, ffn_w_out = ffn_w_in.astype(BF16), ffn_w_out.astype(BF16)
    a_w_in, a_w_out = a_w_in.astype(BF16), a_w_out.astype(BF16)
    b_w_in, b_w_out = b_w_in.astype(BF16), b_w_out.astype(BF16)
    mixer_layers = [l for l in range(depth) if l % 2 == 0]
    for l in range(depth):
        li = l // 2
        need_in = any(j >= l for j in mixer_layers)
        need_out = any(j > l for j in mixer_layers)
        rows_in = rows_all if need_in else rows_x
        rows_out = rows_all if need_out else rows_x
        mod, ng = mods[l], norm_g[l]
        h = _ffn(h, mod, ng, ffn_w_in, ffn_w_out, (l, 0),
                 mi=0, gi=0, rows_out=rows_in, tail=h_tail if (l == 0 and need_in) else None, **common)
        if l % 2 == 0:
            xm, z = _mlstm_inproj(h, mod, ng, a_w_in, (li,), **common)
            wqkv = a_w_qkv[li]
            wq, wk, wv = (_dense_blockdiag(wqkv[t]) for t in range(3))
            wkt = _dense_blockdiag(wqkv[1], transposed=True)
            wg = jnp.pad(a_w_gates[li].reshape(3, di, -1),
                         ((0, 0), (0, 0), (0, GATE_PAD - a_w_gates.shape[2]))).astype(BF16)
            xc, q, kt, v, g, gt = _mlstm_conv_qkv(
                xm, a_conv_w[li].reshape(9, di), a_conv_b[li], wq, wk, wkt, wv, wg, a_b_gates[li],
                ctx_len=ctx_len, **common)
            hf, hb = _mlstm_scan(q, kt, v, g, gt, ctx_len=ctx_len, **common)
            h = _mlstm_out(hf, hb, xc, z, h, mod, ng, a_norm_g[li], a_skip[li], a_w_out, (li,),
                           rows_out=rows_out, **common)
        else:
            h = _gmlp(h, mod, ng, b_w_in, b_b_in[li], b_ln_g[li], b_ln_b[li],
                      b_w_s[li].astype(BF16), b_b_s[li].T, b_w_out, b_b_out[li], (li,),
                      rows_out=rows_out, **common)
        h = _ffn(h, mod, ng, ffn_w_in, ffn_w_out, (l, 1),
                 mi=6, gi=4, rows_out=rows_out, **common)
    return h.reshape(n_batch, seq, d)
```

```python
import functools

import jax
import jax.numpy as jnp
from jax import lax
from jax.experimental import pallas as pl
from jax.experimental.pallas import tpu as pltpu

F32 = jnp.float32
BF16 = jnp.bfloat16

EPS = 1e-6
N_MOD = 9
GRID_W = 64
A_HEADS = 4
A_QKV_BLOCK = 4
B_CHUNK = 128
B_GROUPS = 8

ROW_TILE = 512
MIX_TILE = 512
GMLP_TILE = 512
GMLP_SUB = 256
SCAN_CHUNK = 256
SCAN_HEADS_PER_STEP = 4
FF_CHUNK = 256
QKV_GROUP = 256
GATE_PAD = 128
VMEM_LIMIT = 56 * 1024 * 1024


def _params(n_axes):
    return pltpu.CompilerParams(dimension_semantics=("arbitrary",) * n_axes,
                                vmem_limit_bytes=VMEM_LIMIT)


def _const_spec(shape):
    nd = len(shape)
    return pl.BlockSpec(shape, lambda *_: (0,) * nd, pipeline_mode=pl.Buffered(1))


def _layer_spec(stack, idx):
    lead = len(idx)
    return pl.BlockSpec((None,) * lead + tuple(stack.shape[lead:]), lambda *_: tuple(idx) + (0, 0),
                        pipeline_mode=pl.Buffered(1))


def _rms(x, g):
    return x * lax.rsqrt(jnp.mean(x * x, axis=-1, keepdims=True) + EPS) * g


def _modulate(x, g, shift, scale):
    return _rms(x, g * (1.0 + scale)) + shift


def _silu(x):
    return x * jax.nn.sigmoid(x)


def _mod_index(tile_rows, rows_per_batch, n_batch):
    tiles_per_batch = rows_per_batch // tile_rows

    def index(i):
        return jnp.minimum(i // tiles_per_batch, n_batch)

    return index


def _ada_kernel(c_ref, w_ref, b_ref, o_ref):
    c = c_ref[...]
    a = _silu(c).astype(BF16)
    o_ref[...] = jnp.dot(a, w_ref[...].astype(BF16), preferred_element_type=F32) + b_ref[...]


def _ada_table(cond, w_ada, b_ada):
    depth, d, nd = w_ada.shape
    tn = 1024
    return pl.pallas_call(
        _ada_kernel,
        grid=(depth, nd // tn),
        in_specs=[pl.BlockSpec((8, d), lambda l, j: (0, 0)),
                  pl.BlockSpec((None, d, tn), lambda l, j: (l, 0, j)),
                  pl.BlockSpec((None, 1, tn), lambda l, j: (l, 0, j))],
        out_specs=pl.BlockSpec((None, 8, tn), lambda l, j: (l, 0, j)),
        out_shape=jax.ShapeDtypeStruct((depth, 8, nd), F32),
        compiler_params=_params(2),
        name="ada_table",
    )(cond, w_ada, b_ada.reshape(depth, 1, nd))


def _ffn_kernel(x_ref, *rest, mi, gi, dff, head_tiles):
    if head_tiles is None:
        m_ref, g_ref, win_ref, wout_ref, o_ref, acc_ref = rest
        x = x_ref[...]
    else:
        tail_ref, m_ref, g_ref, win_ref, wout_ref, o_ref, acc_ref = rest
        x = jnp.where(pl.program_id(0) < head_tiles, x_ref[...], tail_ref[...])
    shift, scale, gate = m_ref[mi:mi + 1, :], m_ref[mi + 1:mi + 2, :], m_ref[mi + 2:mi + 3, :]
    xb = _modulate(x, g_ref[gi:gi + 1, :], shift, scale).astype(BF16)
    for c in range(dff // FF_CHUNK):
        lo = c * FF_CHUNK
        hg = jnp.dot(xb, win_ref[:, lo:lo + FF_CHUNK], preferred_element_type=F32)
        hu = jnp.dot(xb, win_ref[:, dff + lo:dff + lo + FF_CHUNK], preferred_element_type=F32)
        a = (_silu(hg) * hu).astype(BF16)
        part = jnp.dot(a, wout_ref[lo:lo + FF_CHUNK, :], preferred_element_type=F32)
        if c == 0:
            acc_ref[...] = part
        else:
            acc_ref[...] += part
    o_ref[...] = x + _rms(acc_ref[...], 0.5 * gate * g_ref[gi + 1:gi + 2, :])


def _ffn(x, mod, ng, w_in, w_out, widx, *, mi, gi, rows_out, rows_per_batch, n_batch, tail=None):
    d = x.shape[1]
    dff = w_out.shape[-2]
    tm = ROW_TILE
    midx = _mod_index(tm, rows_per_batch, n_batch)
    if tail is None:
        head_tiles = None
        rows = [x]
        row_specs = [pl.BlockSpec((tm, d), lambda i: (i, 0))]
    else:
        head_tiles = x.shape[0] // tm
        rows = [x, tail]
        row_specs = [pl.BlockSpec((tm, d), lambda i: (jnp.minimum(i, head_tiles - 1), 0)),
                     pl.BlockSpec((tm, d), lambda i: (jnp.maximum(i - head_tiles, 0), 0))]
    return pl.pallas_call(
        functools.partial(_ffn_kernel, mi=mi, gi=gi, dff=dff, head_tiles=head_tiles),
        grid=(rows_out // tm,),
        in_specs=row_specs + [
                  pl.BlockSpec((None, N_MOD, d), lambda i: (midx(i), 0, 0)),
                  _const_spec(ng.shape), _layer_spec(w_in, widx), _layer_spec(w_out, widx)],
        out_specs=pl.BlockSpec((tm, d), lambda i: (i, 0)),
        out_shape=jax.ShapeDtypeStruct((rows_out, d), F32),
        scratch_shapes=[pltpu.VMEM((tm, d), F32)],
        compiler_params=_params(1),
        name="ffn_half_step",
    )(*rows, mod, ng, w_in, w_out)


def _inproj_kernel(x_ref, m_ref, g_ref, w_ref, xm_ref, z_ref, *, di):
    x = x_ref[...]
    shift, scale = m_ref[3:4, :], m_ref[4:5, :]
    xb = _modulate(x, g_ref[2:3, :], shift, scale).astype(BF16)
    ck = 1024
    for c in range(di // ck):
        xm_ref[:, c * ck:(c + 1) * ck] = jnp.dot(xb, w_ref[:, c * ck:(c + 1) * ck],
                                                 preferred_element_type=F32).astype(BF16)
        z_ref[:, c * ck:(c + 1) * ck] = jnp.dot(xb, w_ref[:, di + c * ck:di + (c + 1) * ck],
                                                preferred_element_type=F32).astype(BF16)


def _mlstm_inproj(x, mod, ng, w_in, widx, *, rows_per_batch, n_batch):
    r, d = x.shape
    di = w_in.shape[-1] // 2
    tm = ROW_TILE
    midx = _mod_index(tm, rows_per_batch, n_batch)
    return pl.pallas_call(
        functools.partial(_inproj_kernel, di=di),
        grid=(r // tm,),
        in_specs=[pl.BlockSpec((tm, d), lambda i: (i, 0)),
                  pl.BlockSpec((None, N_MOD, d), lambda i: (midx(i), 0, 0)),
                  _const_spec(ng.shape), _layer_spec(w_in, widx)],
        out_specs=[pl.BlockSpec((tm, di), lambda i: (i, 0)), pl.BlockSpec((tm, di), lambda i: (i, 0))],
        out_shape=[jax.ShapeDtypeStruct((r, di), BF16), jax.ShapeDtypeStruct((r, di), BF16)],
        compiler_params=_params(1),
        name="mlstm_inproj",
    )(x, mod, ng, w_in)


def _conv_qkv_kernel(xm_ref, top_ref, bot_ref, cw_ref, cb_ref, wq_ref, wk_ref, wkt_ref, wv_ref,
                     wg_ref, bg_ref, bgc_ref,
                     xc_ref, q_ref, kt_ref, v_ref, g_ref, gt_ref,
                     ext_ref, gacc_ref, *, tm, x_tiles, tiles_per_batch, ctx_len, qscale):
    i = pl.program_id(0)
    di = xm_ref.shape[1]
    is_ctx = i >= x_tiles
    jloc = i % tiles_per_batch
    top_ok = jnp.logical_and(jnp.logical_not(is_ctx), jloc > 0)
    bot_ok = jnp.logical_and(jnp.logical_not(is_ctx), jloc < tiles_per_batch - 1)
    w = GRID_W
    pad = 8
    base = pad + w
    ext_ref[0:pad, :] = jnp.zeros((pad, di), F32)
    ext_ref[pad:base, :] = top_ref[...].astype(F32) * jnp.where(top_ok, 1.0, 0.0).astype(F32)
    ext_ref[base:base + tm, :] = xm_ref[...].astype(F32)
    ext_ref[base + tm:base + tm + w, :] = bot_ref[...].astype(F32) * jnp.where(bot_ok, 1.0, 0.0).astype(F32)
    ext_ref[base + tm + w:base + tm + w + pad, :] = jnp.zeros((pad, di), F32)

    gq = QKV_GROUP
    r_idx = lax.broadcasted_iota(jnp.int32, (tm, gq), 0)
    col = jnp.where(is_ctx, r_idx & (ctx_len - 1), r_idx & (w - 1))
    last = jnp.where(is_ctx, ctx_len - 1, w - 1)
    left_ok = (col != 0).astype(F32)
    right_ok = (col != last).astype(F32)
    vert = jnp.where(is_ctx, 0.0, 1.0).astype(F32)

    for g in range(di // gq):
        cs = slice(g * gq, (g + 1) * gq)

        def tap(dy, dx):
            o = base + w * dy + dx
            return ext_ref[o:o + tm, cs]

        def wt(ky, kx):
            row = cw_ref[3 * ky + kx:3 * ky + kx + 1, cs]
            return row if ky == 1 else vert * row

        mid = wt(1, 1) * tap(0, 0) + wt(0, 1) * tap(-1, 0) + wt(2, 1) * tap(1, 0)
        lft = wt(1, 0) * tap(0, -1) + wt(0, 0) * tap(-1, -1) + wt(2, 0) * tap(1, -1)
        rgt = wt(1, 2) * tap(0, 1) + wt(0, 2) * tap(-1, 1) + wt(2, 2) * tap(1, 1)
        conv = mid + left_ok * lft + right_ok * rgt + cb_ref[:, cs]
        xc = _silu(conv)
        xcb = xc.astype(BF16)
        xc_ref[:, cs] = xcb
        xmb = xm_ref[:, cs]
        q = jnp.dot(xcb, wq_ref[g], preferred_element_type=F32)
        k = jnp.dot(xcb, wk_ref[g], preferred_element_type=F32)
        v = jnp.dot(xmb, wv_ref[g], preferred_element_type=F32)
        kt = lax.dot_general(wkt_ref[g], xcb, (((1,), (1,)), ((), ())), preferred_element_type=F32)
        qb, kb, vb = q.astype(BF16), k.astype(BF16), v.astype(BF16)
        q_ref[:, cs] = (q * qscale).astype(BF16)
        v_ref[:, cs] = vb
        kt_ref[cs, :] = kt.astype(BF16)
        part = (jnp.dot(qb, wg_ref[0, cs, :], preferred_element_type=F32)
                + jnp.dot(kb, wg_ref[1, cs, :], preferred_element_type=F32)
                + jnp.dot(vb, wg_ref[2, cs, :], preferred_element_type=F32))
        if g == 0:
            gacc_ref[...] = part
        else:
            gacc_ref[...] += part
    ng = g_ref.shape[1]
    acc = gacc_ref[...]
    g_ref[...] = acc[:, 0:ng] + bg_ref[...]
    gt_ref[...] = acc.T[0:ng, :] + bgc_ref[...]


def _mlstm_conv_qkv(xm, conv_w9, conv_b, wq, wk, wkt, wv, wg, bg, *, rows_per_batch, n_batch, ctx_len):
    r, di = xm.shape
    tm = MIX_TILE
    w = GRID_W
    x_tiles = n_batch * rows_per_batch // tm
    tiles_per_batch = rows_per_batch // tm
    hb = tm // w
    last_halo = r // w - 1
    ng = bg.shape[0]
    dh = di // A_HEADS
    kern = functools.partial(_conv_qkv_kernel, tm=tm, x_tiles=x_tiles, tiles_per_batch=tiles_per_batch,
                             ctx_len=ctx_len, qscale=float(dh) ** -0.5)
    row = pl.BlockSpec((tm, di), lambda i: (i, 0))
    return pl.pallas_call(
        kern,
        grid=(r // tm,),
        in_specs=[row,
                  pl.BlockSpec((w, di), lambda i: (jnp.maximum(i * hb - 1, 0), 0)),
                  pl.BlockSpec((w, di), lambda i: (jnp.minimum((i + 1) * hb, last_halo), 0)),
                  _const_spec(conv_w9.shape), _const_spec((1, di)),
                  _const_spec(wq.shape), _const_spec(wk.shape), _const_spec(wkt.shape),
                  _const_spec(wv.shape), _const_spec(wg.shape),
                  _const_spec((1, ng)), _const_spec((ng, 1))],
        out_specs=[row, row, pl.BlockSpec((di, tm), lambda i: (0, i)), row,
                   pl.BlockSpec((tm, ng), lambda i: (i, 0)), pl.BlockSpec((ng, tm), lambda i: (0, i))],
        out_shape=[jax.ShapeDtypeStruct((r, di), BF16),
                   jax.ShapeDtypeStruct((r, di), BF16),
                   jax.ShapeDtypeStruct((di, r), BF16),
                   jax.ShapeDtypeStruct((r, di), BF16),
                   jax.ShapeDtypeStruct((r, ng), F32),
                   jax.ShapeDtypeStruct((ng, r), F32)],
        scratch_shapes=[pltpu.VMEM((tm + 2 * w + 16, di), F32), pltpu.VMEM((tm, GATE_PAD), F32)],
        compiler_params=_params(1),
        name="mlstm_conv_qkv",
    )(xm, xm, xm, conv_w9, conv_b.reshape(1, di), wq, wk, wkt, wv, wg, bg.reshape(1, ng), bg.reshape(ng, 1))


def _scan_chain(q, kt, v, i_col, f_col, i_row, f_row, vis, vist, neg, c_ref, cb_ref, n_ref, m_ref):
    L = q.shape[0]
    lf_col = jax.nn.log_sigmoid(f_col)
    lf_row = jax.nn.log_sigmoid(f_row)
    b_col = jnp.sum(vis * lf_row, axis=1, keepdims=True)
    b_row = jnp.sum(vist * lf_col, axis=0, keepdims=True)
    b_tot = jnp.sum(lf_row, axis=1, keepdims=True)

    m_prev = m_ref[0:1, 0:1]
    log_d = (b_col - b_row + i_row) + neg
    m_inter = b_col + m_prev
    m_t = jnp.maximum(m_inter, jnp.max(log_d, axis=1, keepdims=True))
    dmat = jnp.exp(log_d - m_t)
    s = jnp.dot(q, kt, preferred_element_type=F32) * dmat
    scale_inter = jnp.exp(m_inter - m_t)
    inter = jnp.dot(q, cb_ref[...], preferred_element_type=F32)
    qn = jnp.dot(q, n_ref[...].astype(BF16), preferred_element_type=F32)[:, 0:1]
    num = jnp.dot(s.astype(BF16), v, preferred_element_type=F32) + scale_inter * inter
    den = jnp.sum(s, axis=1, keepdims=True) + scale_inter * qn
    h = (num / jnp.maximum(jnp.abs(den), jnp.exp(-m_t))).astype(BF16)

    log_w = b_tot - b_col + i_col
    m_new = jnp.maximum(b_tot + m_prev, jnp.max(log_w, axis=0, keepdims=True))
    wgt = jnp.exp(log_w - m_new)
    decay = jnp.exp(b_tot + m_prev - m_new)
    wv = (wgt * v.astype(F32)).astype(BF16)
    c_new = decay * c_ref[...] + jnp.dot(kt, wv, preferred_element_type=F32)
    c_ref[...] = c_new
    cb_ref[...] = c_new.astype(BF16)
    wb = jnp.broadcast_to(wgt, (L, n_ref.shape[1])).astype(BF16)
    n_ref[...] = decay * n_ref[...] + jnp.dot(kt, wb, preferred_element_type=F32)
    m_ref[...] = jnp.broadcast_to(m_new, m_ref.shape)
    return h


def _scan_kernel(qf_ref, ktf_ref, vf_ref, gf_ref, gtf_ref, qb_ref, ktb_ref, vb_ref, gb_ref, gtb_ref,
                 vis_ref, neg_ref, hf_ref, hb_ref, c_ref, cb_ref, n_ref, m_ref, *, hp):
    hg = pl.program_id(1)
    j = pl.program_id(2)

    @pl.when(j == 0)
    def _():
        c_ref[...] = jnp.zeros_like(c_ref)
        cb_ref[...] = jnp.zeros_like(cb_ref)
        n_ref[...] = jnp.zeros_like(n_ref)
        m_ref[...] = jnp.zeros_like(m_ref)

    L = qf_ref.shape[0]
    dh = qf_ref.shape[1] // hp
    streams = ((qf_ref, ktf_ref, vf_ref, gf_ref, gtf_ref, hf_ref), (qb_ref, ktb_ref, vb_ref, gb_ref, gtb_ref, hb_ref))
    for dr, (q_ref, kt_ref, v_ref, g_ref, gt_ref, h_ref) in enumerate(streams):
        g = g_ref[...]
        lane = lax.broadcasted_iota(jnp.int32, g.shape, 1)
        for hh in range(hp):
            if hp == A_HEADS:
                i_idx = dr * 2 * A_HEADS + hh
                f_idx = i_idx + A_HEADS
                i_col, f_col = g[:, i_idx:i_idx + 1], g[:, f_idx:f_idx + 1]
                i_row, f_row = gt_ref[i_idx:i_idx + 1, :], gt_ref[f_idx:f_idx + 1, :]
            else:
                i_idx = dr * 2 * A_HEADS + hg * hp + hh
                f_idx = i_idx + A_HEADS
                i_col = jnp.sum(jnp.where(lane == i_idx, g, 0.0), axis=1, keepdims=True)
                f_col = jnp.sum(jnp.where(lane == f_idx, g, 0.0), axis=1, keepdims=True)
                i_row = gt_ref[pl.ds(i_idx, 1), :]
                f_row = gt_ref[pl.ds(f_idx, 1), :]
            cs = slice(hh * dh, (hh + 1) * dh)
            h_ref[:, cs] = _scan_chain(
                q_ref[:, cs], kt_ref[cs, :], v_ref[:, cs], i_col, f_col, i_row, f_row,
                vis_ref[dr], vis_ref[1 - dr], neg_ref[dr],
                c_ref.at[dr, hh], cb_ref.at[dr, hh], n_ref.at[dr, hh], m_ref.at[dr, hh])


def _mlstm_scan(q, kt, v, g, gt, *, rows_per_batch, n_batch, ctx_len):
    r, di = q.shape
    L = SCAN_CHUNK
    hp = SCAN_HEADS_PER_STEP
    dh = di // A_HEADS
    nctx = ctx_len // L
    nx = rows_per_batch // L
    ctx_base = n_batch * nx
    ngate = g.shape[1]

    def blk_f(b, j):
        return jnp.where(j < nctx, ctx_base + b * nctx + j, b * nx + (j - nctx))

    def blk_b(b, j):
        return jnp.where(j < nctx, ctx_base + b * nctx + (nctx - 1 - j), b * nx + (nx - 1 - (j - nctx)))

    def stream_specs(blk):
        tok = pl.BlockSpec((L, hp * dh), lambda b, h, j: (blk(b, j), h))
        return [tok, pl.BlockSpec((hp * dh, L), lambda b, h, j: (h, blk(b, j))), tok,
                pl.BlockSpec((L, ngate), lambda b, h, j: (blk(b, j), 0)),
                pl.BlockSpec((ngate, L), lambda b, h, j: (0, blk(b, j)))]

    tri = jnp.tril(jnp.ones((L, L), F32))
    vis = jnp.stack([tri, tri.T])
    neg = jnp.where(vis > 0, 0.0, -jnp.inf).astype(F32)
    out_f = pl.BlockSpec((L, hp * dh), lambda b, h, j: (blk_f(b, j), h))
    out_b = pl.BlockSpec((L, hp * dh), lambda b, h, j: (blk_b(b, j), h))
    return pl.pallas_call(
        functools.partial(_scan_kernel, hp=hp),
        grid=(n_batch, A_HEADS // hp, nctx + nx),
        in_specs=stream_specs(blk_f) + stream_specs(blk_b) + [_const_spec(vis.shape), _const_spec(neg.shape)],
        out_specs=[out_f, out_b],
        out_shape=[jax.ShapeDtypeStruct((r, di), BF16), jax.ShapeDtypeStruct((r, di), BF16)],
        scratch_shapes=[pltpu.VMEM((2, hp, dh, dh), F32), pltpu.VMEM((2, hp, dh, dh), BF16),
                        pltpu.VMEM((2, hp, dh, GATE_PAD), F32), pltpu.VMEM((2, hp, 8, 128), F32)],
        compiler_params=_params(3),
        name="mlstm_scan",
    )(q, kt, v, g, gt, q, kt, v, g, gt, vis, neg)


def _mlstm_out_kernel(hf_ref, hb_ref, xc_ref, z_ref, x_ref, m_ref, g_ref, ag_ref, sk_ref, w_ref, o_ref, acc_ref):
    di = xc_ref.shape[1]
    dh = di // A_HEADS
    for hd in range(A_HEADS):
        cs = slice(hd * dh, (hd + 1) * dh)
        h = hf_ref[:, cs].astype(F32) + hb_ref[:, cs].astype(F32)
        mu = jnp.mean(h, axis=-1, keepdims=True)
        hc = h - mu
        var = jnp.mean(hc * hc, axis=-1, keepdims=True)
        hn = hc * lax.rsqrt(var + EPS) * ag_ref[:, cs]
        t = ((hn + sk_ref[:, cs] * xc_ref[:, cs].astype(F32)) * _silu(z_ref[:, cs].astype(F32))).astype(BF16)
        part = jnp.dot(t, w_ref[cs, :], preferred_element_type=F32)
        if hd == 0:
            acc_ref[...] = part
        else:
            acc_ref[...] += part
    o_ref[...] = x_ref[...] + _rms(acc_ref[...], m_ref[5:6, :] * g_ref[3:4, :])


def _mlstm_out(hf, hb, xc, z, x, mod, ng, a_norm_g, a_skip, w_out, widx, *, rows_out, rows_per_batch, n_batch):
    di = xc.shape[1]
    d = x.shape[1]
    tm = MIX_TILE
    midx = _mod_index(tm, rows_per_batch, n_batch)
    row_di = pl.BlockSpec((tm, di), lambda i: (i, 0))
    row_d = pl.BlockSpec((tm, d), lambda i: (i, 0))
    return pl.pallas_call(
        _mlstm_out_kernel,
        grid=(rows_out // tm,),
        in_specs=[row_di, row_di, row_di, row_di, row_d,
                  pl.BlockSpec((None, N_MOD, d), lambda i: (midx(i), 0, 0)),
                  _const_spec(ng.shape), _const_spec((1, di)), _const_spec((1, di)), _layer_spec(w_out, widx)],
        out_specs=row_d,
        out_shape=jax.ShapeDtypeStruct((rows_out, d), F32),
        scratch_shapes=[pltpu.VMEM((tm, d), F32)],
        compiler_params=_params(1),
        name="mlstm_out",
    )(hf, hb, xc, z, x, mod, ng, a_norm_g.reshape(1, di), a_skip.reshape(1, di), w_out)


def _gelu2(x):
    return x * (1.0 + lax.erf(x * (2.0 ** -0.5)))


def _gmlp_kernel(x_ref, m_ref, g_ref, win_ref, bin_ref, lng_ref, lnb_ref, ws_ref, bs_ref, wout_ref,
                 bout_ref, o_ref, *scr, e):
    tm = x_ref.shape[0]
    shift, scale = m_ref[3:4, :], m_ref[4:5, :]
    ck = 512
    ge = e // B_GROUPS
    sub = GMLP_SUB
    nsub = tm // sub

    def p1(sc):
        u_ref, v_ref, t_ref = scr[3 * sc:3 * sc + 3]
        x = x_ref[sc * sub:(sc + 1) * sub, :]
        xb = _modulate(x, g_ref[2:3, :], shift, scale).astype(BF16)
        for c in range(e // ck):
            cs = slice(c * ck, (c + 1) * ck)
            cs2 = slice(e + c * ck, e + (c + 1) * ck)
            u_ref[:, cs] = _gelu2(jnp.dot(xb, win_ref[:, cs], preferred_element_type=F32) + bin_ref[:, cs])
            v_ref[:, cs] = _gelu2(jnp.dot(xb, win_ref[:, cs2], preferred_element_type=F32) + bin_ref[:, cs2])

    def p2(sc):
        u_ref, v_ref, t_ref = scr[3 * sc:3 * sc + 3]
        v = v_ref[...]
        mu = jnp.mean(v, axis=-1, keepdims=True)
        vc = v - mu
        var = jnp.mean(vc * vc, axis=-1, keepdims=True)
        v_ref[...] = vc * lax.rsqrt(var + 4.0 * EPS) * lng_ref[...] + lnb_ref[...]

    def p3(sc):
        u_ref, v_ref, t_ref = scr[3 * sc:3 * sc + 3]
        for rc in range(sub // B_CHUNK):
            rs = slice(rc * B_CHUNK, (rc + 1) * B_CHUNK)
            for g in range(B_GROUPS):
                cs = slice(g * ge, (g + 1) * ge)
                vb = jnp.dot(ws_ref[g], v_ref[rs, cs].astype(BF16), preferred_element_type=F32) + bs_ref[:, g:g + 1]
                t_ref[rs, cs] = (u_ref[rs, cs] * vb).astype(BF16)

    def p4(sc):
        u_ref, v_ref, t_ref = scr[3 * sc:3 * sc + 3]
        rows = slice(sc * sub, (sc + 1) * sub)
        y = jnp.dot(t_ref[...], wout_ref[...], preferred_element_type=F32) + bout_ref[...]
        o_ref[rows, :] = x_ref[rows, :] + _rms(y, m_ref[5:6, :] * g_ref[3:4, :])

    p1(0)
    for sc in range(nsub):
        if sc + 1 < nsub:
            p1(sc + 1)
        p2(sc)
        p3(sc)
        p4(sc)


def _gmlp(x, mod, ng, w_in, b_in, ln_g, ln_b, w_s, b_s_t, w_out, b_out, widx, *, rows_out, rows_per_batch,
          n_batch):
    d = x.shape[1]
    e = w_out.shape[-2]
    tm = GMLP_TILE
    midx = _mod_index(tm, rows_per_batch, n_batch)
    row_d = pl.BlockSpec((tm, d), lambda i: (i, 0))
    return pl.pallas_call(
        functools.partial(_gmlp_kernel, e=e),
        grid=(rows_out // tm,),
        in_specs=[row_d, pl.BlockSpec((None, N_MOD, d), lambda i: (midx(i), 0, 0)),
                  _const_spec(ng.shape), _layer_spec(w_in, widx), _const_spec((1, 2 * e)),
                  _const_spec((1, e)), _const_spec((1, e)), _const_spec(w_s.shape), _const_spec(b_s_t.shape),
                  _layer_spec(w_out, widx), _const_spec((1, d))],
        out_specs=row_d,
        out_shape=jax.ShapeDtypeStruct((rows_out, d), F32),
        scratch_shapes=[pltpu.VMEM((GMLP_SUB, e), F32), pltpu.VMEM((GMLP_SUB, e), F32),
                        pltpu.VMEM((GMLP_SUB, e), BF16)] * (tm // GMLP_SUB),
        compiler_params=_params(1),
        name="gmlp_mixer",
    )(x, mod, ng, w_in, b_in.reshape(1, 2 * e), ln_g.reshape(1, e), ln_b.reshape(1, e), w_s, b_s_t,
      w_out, b_out.reshape(1, d))


def _dense_blockdiag(w, transposed=False):
    nb, bo, bi = w.shape
    per = QKV_GROUP // bi
    wg = w.reshape(nb // per, per, bo, bi)
    if transposed:
        base = wg.reshape(nb // per, per * bo, bi)
        blk = bi
    else:
        base = jnp.swapaxes(wg, 2, 3).reshape(nb // per, per * bi, bo)
        blk = bo
    tiled = jnp.tile(base, (1, 1, per))
    rows = lax.broadcasted_iota(jnp.int32, tiled.shape[1:], 0) // (tiled.shape[1] // per)
    cols = lax.broadcasted_iota(jnp.int32, tiled.shape[1:], 1) // blk
    return jnp.where(rows == cols, tiled, 0.0).astype(BF16)


def kernel(x, c, ctx, c_ctx, w_ada, b_ada, norm_g, ffn_w_in, ffn_w_out, a_w_in, a_conv_w, a_conv_b, a_w_qkv, a_w_gates, a_b_gates, a_norm_g, a_skip, a_w_out, b_w_in, b_b_in, b_ln_g, b_ln_b, b_w_s, b_b_s, b_w_out, b_b_out):
    n_batch, seq, d = x.shape
    ctx_len = ctx.shape[1]
    depth = w_ada.shape[0]
    rows_x = n_batch * seq
    rows_all = rows_x + n_batch * ctx_len
    di = a_w_out.shape[1]
    assert seq % ROW_TILE == 0 and (n_batch * ctx_len) % ROW_TILE == 0 and (n_batch * ctx_len) % MIX_TILE == 0
    assert MIX_TILE % ctx_len == 0 or ctx_len % MIX_TILE == 0
    assert seq % GRID_W == 0 and MIX_TILE % GRID_W == 0 and ctx_len % SCAN_CHUNK == 0
    assert ctx_len & (ctx_len - 1) == 0 and a_w_gates.shape[2] == 4 * A_HEADS

    cond = jnp.concatenate([c, c_ctx[None], jnp.zeros((8 - n_batch - 1, d), F32)], axis=0)
    mods = _ada_table(cond, w_ada, b_ada).reshape(depth, 8, N_MOD, d)

    h = x.reshape(rows_x, d)
    h_tail = ctx.reshape(n_batch * ctx_len, d)
    common = dict(rows_per_batch=seq, n_batch=n_batch)
    ffn_w_in, ffn_w_out = ffn_w_in.astype(BF16), ffn_w_out.astype(BF16)
    a_w_in, a_w_out = a_w_in.astype(BF16), a_w_out.astype(BF16)
    b_w_in, b_w_out = b_w_in.astype(BF16), b_w_out.astype(BF16)
    mixer_layers = [l for l in range(depth) if l % 2 == 0]
    for l in range(depth):
        li = l // 2
        need_in = any(j >= l for j in mixer_layers)
        need_out = any(j > l for j in mixer_layers)
        rows_in = rows_all if need_in else rows_x
        rows_out = rows_all if need_out else rows_x
        mod, ng = mods[l], norm_g[l]
        h = _ffn(h, mod, ng, ffn_w_in, ffn_w_out, (l, 0),
                 mi=0, gi=0, rows_out=rows_in, tail=h_tail if (l == 0 and need_in) else None, **common)
        if l % 2 == 0:
            xm, z = _mlstm_inproj(h, mod, ng, a_w_in, (li,), **common)
            wqkv = a_w_qkv[li]
            wq, wk, wv = (_dense_blockdiag(wqkv[t]) for t in range(3))
            wkt = _dense_blockdiag(wqkv[1], transposed=True)
            wg = jnp.pad(a_w_gates[li].reshape(3, di, -1),
                         ((0, 0), (0, 0), (0, GATE_PAD - a_w_gates.shape[2]))).astype(BF16)
            xc, q, kt, v, g, gt = _mlstm_conv_qkv(
                xm, a_conv_w[li].reshape(9, di), a_conv_b[li], wq, wk, wkt, wv, wg, a_b_gates[li],
                ctx_len=ctx_len, **common)
            hf, hb = _mlstm_scan(q, kt, v, g, gt, ctx_len=ctx_len, **common)
            h = _mlstm_out(hf, hb, xc, z, h, mod, ng, a_norm_g[li], a_skip[li], a_w_out, (li,),
                           rows_out=rows_out, **common)
        else:
            h = _gmlp(h, mod, ng, b_w_in, b_b_in[li], b_ln_g[li], b_ln_b[li],
                      (0.5 * b_w_s[li]).astype(BF16), 0.5 * b_b_s[li].T, b_w_out, b_b_out[li], (li,),
                      rows_out=rows_out, **common)
        h = _ffn(h, mod, ng, ffn_w_in, ffn_w_out, (l, 1),
                 mi=6, gi=4, rows_out=rows_out, **common)
    return h.reshape(n_batch, seq, d)
```
